```python
import math
import jax, jax.numpy as jnp
from jax import lax
import numpy as np

D_MODEL = 1024
BATCH = 8
SEQ = 2048
DEPTH = 4

GRID_W = 64
CTX_LEN = 256
N_MIXERS = 4
EPS = 1e-6
ROPE_BASE = 10000.0

FOURIER_GROUPS = 8
DIFF_HEADS = 8
DIFF_HEAD_DIM = 64
DIFF_Q_BLOCK = 128
SSD_D_INNER = 2 * D_MODEL
SSD_HEAD_DIM = 64
SSD_HEADS = SSD_D_INNER // SSD_HEAD_DIM
SSD_STATE = 128
SSD_GROUPS = 8
SSD_CONV = 5
SSD_CONV_DIM = SSD_D_INNER + 2 * SSD_GROUPS * SSD_STATE
SSD_IN_DIM = SSD_D_INNER + SSD_CONV_DIM + 2 * SSD_HEADS
HGRN_HEAD_DIM = 128
HGRN_HEADS = D_MODEL // HGRN_HEAD_DIM
SCAN_CHUNK = 64
PEER_HEADS = 8
PEER_N_KEYS = 128
PEER_EXPERTS = PEER_N_KEYS ** 2
PEER_KEY_DIM = 256
PEER_TOPK = 16
PEER_BLOCK = 128

kernel_name = "hybrid_prefix_dit_fnet_diffattn_ssd_hgrn2_peer"


def rmsnorm(x, g):
    xf = x.astype(jnp.float32)
    y = xf * lax.rsqrt(jnp.mean(xf * xf, -1, keepdims=True) + EPS)
    return (y * g.astype(jnp.float32)).astype(x.dtype)


def group_rmsnorm(x, g, groups):
    shp = x.shape
    xf = x.astype(jnp.float32).reshape(*shp[:-1], groups, shp[-1] // groups)
    xf = xf * lax.rsqrt(jnp.mean(xf * xf, -1, keepdims=True) + EPS)
    return (xf.reshape(shp) * g.astype(jnp.float32)).astype(x.dtype)


def modulate(h, shift, scale):
    return h * (1 + scale) + shift


def axial_rope_tables(length, head_dim):
    rows = length // GRID_W
    row = jnp.repeat(jnp.arange(rows), GRID_W).astype(jnp.float32)
    col = jnp.tile(jnp.arange(GRID_W), rows).astype(jnp.float32)
    nf = head_dim // 4
    freqs = ROPE_BASE ** (-jnp.arange(nf, dtype=jnp.float32) / nf)
    ang = jnp.stack([row[:, None] * freqs, col[:, None] * freqs], axis=1)
    return jnp.cos(ang), jnp.sin(ang)


def apply_axial_rope(x, cos, sin):
    shp = x.shape
    nf = shp[-1] // 4
    xr = x.astype(jnp.float32).reshape(*shp[:-1], 2, 2, nf)
    bshape = (shp[1],) + (1,) * (x.ndim - 3) + (2, nf)
    c, s = cos.reshape(bshape), sin.reshape(bshape)
    xa, xb = xr[..., 0, :], xr[..., 1, :]
    out = jnp.stack([xa * c - xb * s, xb * c + xa * s], axis=-2)
    return out.reshape(shp).astype(x.dtype)


def depthwise_conv_centred(x, w, b):
    pad = (w.shape[0] - 1) // 2
    y = lax.conv_general_dilated(x, w[:, None, :].astype(x.dtype), window_strides=(1,),
                                 padding=[(pad, pad)], dimension_numbers=('NWC', 'WIO', 'NWC'),
                                 feature_group_count=x.shape[-1])
    return y + b


def chunked_gated_scan(q, k, v, log_f, s0, with_output):
    B, L, H, K = k.shape
    V = v.shape[-1]
    Q = SCAN_CHUNK
    nc = L // Q

    def chunks(a):
        return a.reshape(B, nc, Q, *a.shape[2:]).swapaxes(0, 1)

    causal = jnp.tril(jnp.ones((Q, Q), bool))
    scalar_decay = log_f.shape[-1] == 1

    def body(S, inp):
        qc, kc, vc, gc = inp
        cs = jnp.cumsum(gc.astype(jnp.float32), axis=1)
        last = cs[:, -1]
        k_to_end = kc * jnp.exp(last[:, None] - cs)
        S_new = jnp.exp(last)[..., None] * S + jnp.einsum('bjhk,bjhv->bhkv', k_to_end, vc)
        if not with_output:
            return S_new, None
        seg = cs[:, :, None] - cs[:, None, :]
        decay = jnp.exp(jnp.where(causal[None, :, :, None, None], seg, -jnp.inf))
        if scalar_decay:
            w = jnp.einsum('bihk,bjhk->bijh', qc, kc) * decay[..., 0]
        else:
            w = jnp.einsum('bihk,bjhk,bijhk->bijh', qc, kc, decay)
        o = (jnp.einsum('bijh,bjhv->bihv', w, vc)
             + jnp.einsum('bihk,bhkv->bihv', qc * jnp.exp(cs), S))
        return S_new, o

    s_end, o = lax.scan(body, s0, (chunks(q), chunks(k), chunks(v), chunks(log_f)))
    if with_output:
        o = o.swapaxes(0, 1).reshape(B, L, H, V)
    return s_end, o


def context_then_latent_scan(ctx_in, lat_in, reverse, ctx_out):
    flip = (lambda a: jnp.flip(a, axis=1)) if reverse else (lambda a: a)
    qc, kc, vc, gc = [flip(a) for a in ctx_in]
    ql, kl, vl, gl = [flip(a) for a in lat_in]
    B, _, H, K = kc.shape
    s0 = jnp.zeros((B, H, K, vc.shape[-1]), jnp.float32)
    s_ctx, oc = chunked_gated_scan(qc, kc, vc, gc, s0, ctx_out)
    _, ol = chunked_gated_scan(ql, kl, vl, gl, s_ctx, True)
    return (flip(oc) if ctx_out else None), flip(ol)


def fourier_mixer(hc, hl, w_out, ctx_out):
    def mix(h):
        B, L, _ = h.shape
        hg = h.astype(jnp.float32).reshape(B, L, FOURIER_GROUPS, D_MODEL // FOURIER_GROUPS)
        f = jnp.fft.fft2(hg, axes=(1, 3), norm='ortho').real
        return f.reshape(B, L, D_MODEL).astype(h.dtype) @ w_out
    return (mix(hc) if ctx_out else None), mix(hl)


def diff_attention_mixer(hc, hl, w_in, lam, sub_g, w_out, depth_idx, ctx_out):
    H, Dh = DIFF_HEADS, DIFF_HEAD_DIM
    lam_init = 0.8 - 0.6 * math.exp(-0.3 * depth_idx)
    lamf = lam.astype(jnp.float32)
    lam_full = jnp.exp(jnp.sum(lamf[0] * lamf[1])) - jnp.exp(jnp.sum(lamf[2] * lamf[3])) + lam_init
    scale = Dh ** -0.5

    def proj(h):
        B, L, _ = h.shape
        q, k, v = jnp.split(h @ w_in, 3, axis=-1)
        return (q.reshape(B, L, H, 2, Dh) * scale, k.reshape(B, L, H, 2, Dh), v.reshape(B, L, H, 2 * Dh))

    qc, kc, vc = proj(hc)
    ql, kl, vl = proj(hl)
    B, L = hl.shape[:2]
    cos, sin = axial_rope_tables(L, Dh)
    ql, kl = apply_axial_rope(ql, cos, sin), apply_axial_rope(kl, cos, sin)
    keys = jnp.concatenate([kl, kc], axis=1)
    vals = jnp.concatenate([vl, vc], axis=1)

    def attend(q, k, v):
        s = jnp.einsum('bqhcd,bkhcd->bhcqk', q, k).astype(jnp.float32)
        p = jax.nn.softmax(s, axis=-1)
        a = p[:, :, 0] - lam_full * p[:, :, 1]
        return jnp.einsum('bhqk,bkhv->bqhv', a.astype(v.dtype), v)

    nb = L // DIFF_Q_BLOCK
    qb = ql.reshape(B, nb, DIFF_Q_BLOCK, H, 2, Dh).swapaxes(0, 1)
    ob = lax.map(lambda q: attend(q, keys, vals), qb)
    ol = ob.swapaxes(0, 1).reshape(B, L, H, 2 * Dh)

    def finish(o):
        o = rmsnorm(o, sub_g) * (1 - lam_init)
        return o.reshape(*o.shape[:2], H * 2 * Dh) @ w_out

    return (finish(attend(qc, kc, vc)) if ctx_out else None), finish(ol)


def ssd_mixer(hc, hl, w_in, conv_w, conv_b, a_log, dt_bias, d_skip, norm_g, w_out, ctx_out):
    a = -jnp.exp(a_log.astype(jnp.float32))
    rep = SSD_HEADS // SSD_GROUPS

    def project(h):
        B, L, _ = h.shape
        z, xbc, dt = jnp.split(h @ w_in, [SSD_D_INNER, SSD_D_INNER + SSD_CONV_DIM], axis=-1)
        xbc = jax.nn.silu(depthwise_conv_centred(xbc, conv_w, conv_b))
        xs, bm, cm = jnp.split(xbc, [SSD_D_INNER, SSD_D_INNER + SSD_GROUPS * SSD_STATE], axis=-1)
        xs = xs.reshape(B, L, SSD_HEADS, SSD_HEAD_DIM)
        bm = jnp.repeat(bm.reshape(B, L, SSD_GROUPS, SSD_STATE), rep, axis=2)
        cm = jnp.repeat(cm.reshape(B, L, SSD_GROUPS, SSD_STATE), rep, axis=2)
        dt = jax.nn.softplus(dt.astype(jnp.float32).reshape(B, L, 2, SSD_HEADS) + dt_bias.astype(jnp.float32))
        dirs = [(cm, bm * dt[:, :, d, :, None], xs, (dt[:, :, d] * a[d])[..., None]) for d in range(2)]
        return z, xs, dirs

    zc, xsc, dirs_c = project(hc)
    zl, xsl, dirs_l = project(hl)
    oc_f, ol_f = context_then_latent_scan(dirs_c[0], dirs_l[0], False, ctx_out)
    oc_b, ol_b = context_then_latent_scan(dirs_c[1], dirs_l[1], True, ctx_out)

    def finish(o, xs, z):
        B, L = o.shape[:2]
        y = o + xs.astype(jnp.float32) * d_skip.astype(jnp.float32)[:, None]
        y = y.reshape(B, L, SSD_D_INNER).astype(z.dtype)
        y = group_rmsnorm(y * jax.nn.silu(z), norm_g, SSD_GROUPS)
        return y @ w_out

    yc = finish(oc_f + oc_b, xsc, zc) if ctx_out else None
    return yc, finish(ol_f + ol_b, xsl, zl)


def hgrn2_mixer(hc, hl, w_in, lb, norm_g, w_out, ctx_out):
    lbh = lb.reshape(HGRN_HEADS, HGRN_HEAD_DIM)

    def project(h):
        B, L, _ = h.shape
        q, f_fw, f_bw, i_, g = jnp.split(h @ w_in, 5, axis=-1)
        heads = lambda t: t.reshape(B, L, HGRN_HEADS, HGRN_HEAD_DIM)
        q, v = heads(jax.nn.silu(q)), heads(i_)
        dirs = []
        for f_raw in (f_fw, f_bw):
            f = lbh + (1 - lbh) * jax.nn.sigmoid(heads(f_raw).astype(jnp.float32))
            dirs.append((q, 1 - f, v, jnp.log(f)))
        return g, dirs

    gc, dirs_c = project(hc)
    gl, dirs_l = project(hl)
    oc_f, ol_f = context_then_latent_scan(dirs_c[0], dirs_l[0], False, ctx_out)
    oc_b, ol_b = context_then_latent_scan(dirs_c[1], dirs_l[1], True, ctx_out)

    def finish(o, g):
        B, L = o.shape[:2]
        o = group_rmsnorm(o.reshape(B, L, D_MODEL).astype(g.dtype), norm_g, HGRN_HEADS)
        return (o * jax.nn.silu(g)) @ w_out

    yc = finish(oc_f + oc_b, gc) if ctx_out else None
    return yc, finish(ol_f + ol_b, gl)


def peer_ffn(h, w_q, keys, u, v):
    T, D = h.shape
    q = (h @ w_q).reshape(T, PEER_HEADS, 2, PEER_KEY_DIM // 2)
    s = jnp.einsum('thcd,hcnd->thcn', q, keys).astype(jnp.float32)
    sv, si = lax.top_k(s, PEER_TOPK)
    cand = sv[:, :, 0, :, None] + sv[:, :, 1, None, :]
    best, pos = lax.top_k(cand.reshape(T, PEER_HEADS, PEER_TOPK * PEER_TOPK), PEER_TOPK)
    idx1 = jnp.take_along_axis(si[:, :, 0], pos // PEER_TOPK, axis=-1)
    idx2 = jnp.take_along_axis(si[:, :, 1], pos % PEER_TOPK, axis=-1)
    expert = (idx1 * PEER_N_KEYS + idx2).reshape(T, PEER_HEADS * PEER_TOPK)
    gate = jax.nn.softmax(best, axis=-1).reshape(T, PEER_HEADS * PEER_TOPK)
    nb = T // PEER_BLOCK

    def block(args):
        hb, eb, gb = args
        act = jax.nn.gelu(jnp.einsum('tkd,td->tk', u[eb], hb), approximate=False)
        return jnp.einsum('tk,tkd->td', (gb * act).astype(hb.dtype), v[eb])

    out = lax.map(block, (h.reshape(nb, PEER_BLOCK, D), expert.reshape(nb, PEER_BLOCK, -1),
                          gate.reshape(nb, PEER_BLOCK, -1)))
    return out.reshape(T, D)


def setup_inputs(seed: int = 0) -> dict:
    key = jax.random.key(seed)
    ks = iter(jax.random.split(key, 40))
    D = D_MODEL

    def nrm(shape, scale):
        return jax.random.normal(next(ks), shape, jnp.float32) * scale

    def n_of(kind):
        return len(range(kind, DEPTH, N_MIXERS))

    nA, nB, nC, nD = n_of(0), n_of(1), n_of(2), n_of(3)
    dt0 = jnp.exp(jax.random.uniform(next(ks), (nC, 2, SSD_HEADS), jnp.float32,
                                     math.log(1e-3), math.log(1e-1)))
    return {
        "x": nrm((BATCH, SEQ, D), 1.0),
        "c": nrm((BATCH, D), 1.0),
        "ctx": nrm((BATCH, CTX_LEN, D), 1.0),
        "c_ctx": nrm((D,), 1.0),
        "w_mod": nrm((DEPTH, D, 6 * D), 0.5 * D ** -0.5),
        "b_mod": nrm((DEPTH, 6 * D), 0.02),
        "norm_g": 1.0 + nrm((DEPTH, 2, D), 0.02),
        "four_w_out": nrm((nA, D, D), D ** -0.5),
        "diff_w_in": nrm((nB, D, 3 * D), D ** -0.5),
        "diff_lam": nrm((nB, 4, DIFF_HEAD_DIM), 0.1),
        "diff_norm_g": 1.0 + nrm((nB, 2 * DIFF_HEAD_DIM), 0.02),
        "diff_w_out": nrm((nB, D, D), D ** -0.5),
        "ssd_w_in": nrm((nC, D, SSD_IN_DIM), D ** -0.5),
        "ssd_conv_w": nrm((nC, SSD_CONV, SSD_CONV_DIM), SSD_CONV ** -0.5),
        "ssd_conv_b": nrm((nC, SSD_CONV_DIM), 0.02),
        "ssd_a_log": jnp.log(jax.random.uniform(next(ks), (nC, 2, SSD_HEADS), jnp.float32, 1.0, 16.0)),
        "ssd_dt_bias": dt0 + jnp.log(-jnp.expm1(-dt0)),
        "ssd_d": 1.0 + nrm((nC, SSD_HEADS), 0.1),
        "ssd_norm_g": 1.0 + nrm((nC, SSD_D_INNER), 0.02),
        "ssd_w_out": nrm((nC, SSD_D_INNER, D), SSD_D_INNER ** -0.5),
        "hgrn_w_in": nrm((nD, D, 5 * D), D ** -0.5),
        "hgrn_lb": nrm((DEPTH, D), 0.1),
        "hgrn_norm_g": 1.0 + nrm((nD, D), 0.02),
        "hgrn_w_out": nrm((nD, D, D), D ** -0.5),
        "peer_wq": nrm((DEPTH, D, PEER_HEADS * PEER_KEY_DIM), D ** -0.5),
        "peer_keys": nrm((DEPTH, PEER_HEADS, 2, PEER_N_KEYS, PEER_KEY_DIM // 2), (PEER_KEY_DIM // 2) ** -0.5),
        "peer_u": nrm((DEPTH, PEER_EXPERTS, D), D ** -0.5),
        "peer_v": nrm((DEPTH, PEER_EXPERTS, D), 0.25),
        "final_g": 1.0 + nrm((D,), 0.02),
    }


def reference(x, c, ctx, c_ctx, w_mod, b_mod, norm_g, four_w_out, diff_w_in, diff_lam, diff_norm_g,
              diff_w_out, ssd_w_in, ssd_conv_w, ssd_conv_b, ssd_a_log, ssd_dt_bias, ssd_d, ssd_norm_g,
              ssd_w_out, hgrn_w_in, hgrn_lb, hgrn_norm_g, hgrn_w_out, peer_wq, peer_keys, peer_u, peer_v,
              final_g):
    B, L, D = x.shape
    C = ctx.shape[1]
    lower_bounds = jnp.cumsum(jax.nn.softmax(hgrn_lb.astype(jnp.float32), axis=0), axis=0)
    lower_bounds = lower_bounds - lower_bounds[0]
    xl, xc = x, ctx
    for i in range(DEPTH):
        last = i == DEPTH - 1
        kind, j = i % N_MIXERS, i // N_MIXERS
        mod_l = (jax.nn.silu(c) @ w_mod[i] + b_mod[i])[:, None, :]
        mod_c = (jax.nn.silu(c_ctx) @ w_mod[i] + b_mod[i])[None, None, :]
        sh1_l, sc1_l, ga1_l, sh2_l, sc2_l, ga2_l = jnp.split(mod_l, 6, axis=-1)
        sh1_c, sc1_c, ga1_c, sh2_c, sc2_c, ga2_c = jnp.split(mod_c, 6, axis=-1)

        hl = modulate(rmsnorm(xl, norm_g[i, 0]), sh1_l, sc1_l)
        hc = modulate(rmsnorm(xc, norm_g[i, 0]), sh1_c, sc1_c)
        ctx_out = not last
        if kind == 0:
            yc, yl = fourier_mixer(hc, hl, four_w_out[j], ctx_out)
        elif kind == 1:
            yc, yl = diff_attention_mixer(hc, hl, diff_w_in[j], diff_lam[j], diff_norm_g[j],
                                          diff_w_out[j], i, ctx_out)
        elif kind == 2:
            yc, yl = ssd_mixer(hc, hl, ssd_w_in[j], ssd_conv_w[j], ssd_conv_b[j], ssd_a_log[j],
                               ssd_dt_bias[j], ssd_d[j], ssd_norm_g[j], ssd_w_out[j], ctx_out)
        else:
            yc, yl = hgrn2_mixer(hc, hl, hgrn_w_in[j], lower_bounds[i], hgrn_norm_g[j],
                                 hgrn_w_out[j], ctx_out)
        xl = xl + ga1_l * yl
        hl = modulate(rmsnorm(xl, norm_g[i, 1]), sh2_l, sc2_l)
        if last:
            xl = xl + ga2_l * peer_ffn(hl.reshape(B * L, D), peer_wq[i], peer_keys[i],
                                       peer_u[i], peer_v[i]).reshape(B, L, D)
        else:
            xc = xc + ga1_c * yc
            hc = modulate(rmsnorm(xc, norm_g[i, 1]), sh2_c, sc2_c)
            tokens = jnp.concatenate([hl.reshape(B * L, D), hc.reshape(B * C, D)], axis=0)
            out = peer_ffn(tokens, peer_wq[i], peer_keys[i], peer_u[i], peer_v[i])
            xl = xl + ga2_l * out[:B * L].reshape(B, L, D)
            xc = xc + ga2_c * out[B * L:].reshape(B, C, D)
    return rmsnorm(xl, final_g)
```

```python
import functools
import math

import jax
import jax.numpy as jnp
import numpy as np
from jax import lax
from jax.experimental import pallas as pl
from jax.experimental.pallas import tpu as pltpu

F32 = jnp.float32
BF16 = jnp.bfloat16

EPS = 1e-6
ROPE_BASE = 10000.0
GRID_W = 64
FOURIER_GROUPS = 8
DIFF_HEADS = 8
DIFF_HEAD_DIM = 64
SSD_HEAD_DIM = 64
SSD_STATE = 128
SSD_GROUPS = 8
HGRN_HEAD_DIM = 128
PEER_TOPK = 16

LANES = 128
VMEM_LIMIT_BYTES = 56 * 2**20
MOD_ROWS = 16
EXP_CLAMP = 80.0


def _cparams(*sem):
    return pltpu.CompilerParams(dimension_semantics=sem, vmem_limit_bytes=VMEM_LIMIT_BYTES)


class Dims:
    def __init__(self, B, L, C, D):
        self.B, self.L, self.C, self.D = B, L, C, D
        self.TL = B * L
        self.T = B * L + B * C

    def tile(self, n_rows, pref):
        t = pref
        while self.L % t or n_rows % t:
            t //= 2
        return t

    def mod_row(self, i, tm):
        n_lat = self.TL // tm
        return jnp.where(i < n_lat, i // (self.L // tm), self.B)


def _silu(x):
    return x * (1.0 / (1.0 + jnp.exp(-x)))


def _sigmoid(x):
    return 1.0 / (1.0 + jnp.exp(-x))


def _softplus(x):
    return jnp.maximum(x, 0.0) + jnp.log(1.0 + jnp.exp(-jnp.abs(x)))


def _gelu_exact(x):
    return 0.5 * x * (1.0 + lax.erf(x * (1.0 / math.sqrt(2.0))))


def _norm_mod(x, g, shift, scale):
    y = x * lax.rsqrt(jnp.mean(x * x, axis=-1, keepdims=True) + EPS)
    return (y * g) * (1.0 + scale) + shift


def _group_rmsnorm(y, g, group):
    parts = []
    for s in range(0, y.shape[1], group):
        p = y[:, s:s + group]
        parts.append(p * lax.rsqrt(jnp.mean(p * p, axis=-1, keepdims=True) + EPS))
    return jnp.concatenate(parts, axis=1) * g


def _split3(x):
    a = x.astype(BF16)
    r = x - a.astype(F32)
    b = r.astype(BF16)
    c = (r - b.astype(F32)).astype(BF16)
    return a, b, c


def _dot(a, b):
    return jnp.dot(a, b, preferred_element_type=F32)


def _dot_nt(a, b):
    return lax.dot_general(a, b, (((1,), (1,)), ((), ())), preferred_element_type=F32)


def _dot_tn(a, b):
    return lax.dot_general(a, b, (((0,), (0,)), ((), ())), preferred_element_type=F32)


def _ones_dot(mat01, x):
    a, b, c = _split3(x)
    return _dot(mat01, a) + _dot(mat01, b) + _dot(mat01, c)


def _mod_kernel(c_ref, w_ref, b_ref, o_ref):
    h = _silu(c_ref[...]).astype(BF16)
    o_ref[0] = _dot(h, w_ref[0].astype(BF16)) + b_ref[0]


def _mod_tables(cc, w_mod, b_mod):
    depth, D, N = w_mod.shape
    tn = 1536
    return pl.pallas_call(
        _mod_kernel,
        grid=(depth, N // tn),
        in_specs=[
            pl.BlockSpec((MOD_ROWS, D), lambda l, j: (0, 0)),
            pl.BlockSpec((1, D, tn), lambda l, j: (l, 0, j)),
            pl.BlockSpec((1, 1, tn), lambda l, j: (l, 0, j)),
        ],
        out_specs=pl.BlockSpec((1, MOD_ROWS, tn), lambda l, j: (l, 0, j)),
        out_shape=jax.ShapeDtypeStruct((depth, MOD_ROWS, N), F32),
        compiler_params=_cparams("parallel", "parallel"),
        name="mod_tables",
    )(cc, w_mod, b_mod.reshape(depth, 1, N))


def _proj_kernel(epilogue, n_extra, x_ref, g_ref, sh_ref, sc_ref, w_ref, *rest):
    h = _norm_mod(x_ref[...], g_ref[...], sh_ref[0], sc_ref[0])
    y = _dot(h.astype(BF16), w_ref[...])
    epilogue(y, rest[:n_extra], rest[n_extra:])


def _norm_mod_proj(dims, x, g, shift_tab, scale_tab, w, epilogue, out_shapes, out_specs,
                   extra=(), extra_specs=(), tm=256, name="proj"):
    D = dims.D
    n_rows = x.shape[0]
    tab_spec = pl.BlockSpec((1, 1, D), lambda i: (dims.mod_row(i, tm), 0, 0))
    return pl.pallas_call(
        functools.partial(_proj_kernel, epilogue, len(extra)),
        grid=(n_rows // tm,),
        in_specs=[
            pl.BlockSpec((tm, D), lambda i: (i, 0)),
            pl.BlockSpec((1, D), lambda i: (0, 0)),
            tab_spec, tab_spec,
            pl.BlockSpec(w.shape, lambda i: (0, 0)),
            *extra_specs,
        ],
        out_specs=out_specs,
        out_shape=out_shapes,
        compiler_params=_cparams("parallel"),
        name=name,
    )(x, g.reshape(1, D), shift_tab, scale_tab, w, *extra)


def _out_proj_kernel(prologue, n_in, *refs):
    ins = refs[:n_in]
    w_ref, gate_ref, x_ref, o_ref = refs[n_in:]
    y = prologue(*ins)
    o_ref[...] = x_ref[...] + gate_ref[0] * _dot(y.astype(BF16), w_ref[...])


def _out_proj(dims, n_rows, ins, in_specs, prologue, w, gate_tab, x, tm=256, name="out_proj"):
    D = dims.D
    return pl.pallas_call(
        functools.partial(_out_proj_kernel, prologue, len(ins)),
        grid=(n_rows // tm,),
        in_specs=[
            *in_specs,
            pl.BlockSpec(w.shape, lambda i: (0, 0)),
            pl.BlockSpec((1, 1, D), lambda i: (dims.mod_row(i, tm), 0, 0)),
            pl.BlockSpec((tm, D), lambda i: (i, 0)),
        ],
        out_specs=pl.BlockSpec((tm, D), lambda i: (i, 0)),
        out_shape=jax.ShapeDtypeStruct((n_rows, D), F32),
        compiler_params=_cparams("parallel"),
        name=name,
    )(*ins, w, gate_tab, x)


def _dft_tables(n):
    k = jnp.arange(n, dtype=jnp.int32)
    ang = (2.0 * math.pi / n) * ((k[:, None] * k[None, :]) % n).astype(F32)
    return jnp.cos(ang), jnp.sin(ang)


def _seq_dft_kernel(scale, c_ref, s_ref, h_ref, o_ref):
    n = h_ref.shape[1] // 2
    acc = _dot(c_ref[...], h_ref[:, :n]) - _dot(s_ref[...], h_ref[:, n:])
    o_ref[...] = acc * scale


def _seq_dft(dims, hcs, prev, seq_len, row0, group_width, tm):
    D = dims.D
    cl, sl = _dft_tables(seq_len)
    nblk = seq_len // tm
    blk0 = row0 // tm
    seq0 = row0 // seq_len
    scale = 1.0 / math.sqrt(seq_len * group_width)
    in_specs = [
        pl.BlockSpec((tm, seq_len), lambda b, i: (i, 0)),
        pl.BlockSpec((tm, seq_len), lambda b, i: (i, 0)),
        pl.BlockSpec((seq_len, 2 * D), lambda b, i: (seq0 + b, 0)),
    ]
    args = [cl.astype(BF16), sl.astype(BF16), hcs]
    aliases = {}
    kern = functools.partial(_seq_dft_kernel, scale)
    if prev is not None:
        in_specs.append(pl.BlockSpec(memory_space=pl.ANY))
        args.append(prev)
        aliases = {3: 0}
        kern = lambda c, s, h, _p, o: _seq_dft_kernel(scale, c, s, h, o)
    return pl.pallas_call(
        kern,
        grid=(dims.B, nblk),
        in_specs=in_specs,
        out_specs=pl.BlockSpec((tm, D), lambda b, i: (blk0 + b * nblk + i, 0)),
        out_shape=jax.ShapeDtypeStruct((dims.T, D), F32),
        input_output_aliases=aliases,
        compiler_params=_cparams("parallel", "parallel"),
        name="seq_dft",
    )(*args)


def _fourier_layer(dims, X, g, tabs, w_out):
    D = dims.D
    gw = D // FOURIER_GROUPS
    cg, sg = _dft_tables(gw)
    eye = jnp.eye(FOURIER_GROUPS, dtype=F32)
    wcs = jnp.concatenate([jnp.kron(eye, cg), jnp.kron(eye, sg)], axis=1).astype(BF16)

    def epi(y, extra, outs):
        outs[0][...] = y.astype(BF16)

    hcs = _norm_mod_proj(
        dims, X, g, tabs[0], tabs[1], wcs, epi,
        jax.ShapeDtypeStruct((dims.T, 2 * D), BF16),
        pl.BlockSpec((256, 2 * D), lambda i: (i, 0)), name="fourier_in")
    f = _seq_dft(dims, hcs, None, dims.L, 0, gw, min(512, dims.L))
    f = _seq_dft(dims, hcs, f, dims.C, dims.TL, gw, min(512, dims.C))
    return _out_proj(dims, dims.T, [f], [pl.BlockSpec((256, D), lambda i: (i, 0))],
                     lambda r: r[...], w_out.astype(BF16), tabs[2], X, name="fourier_out")


def _rope_tables(dims, tm):
    L, Dh = dims.L, DIFF_HEAD_DIM
    nf = Dh // 4
    pos = jnp.arange(L)
    row = (pos // GRID_W).astype(F32)
    col = (pos % GRID_W).astype(F32)
    freqs = ROPE_BASE ** (-jnp.arange(nf, dtype=F32) / nf)
    ar, ac = row[:, None] * freqs, col[:, None] * freqs
    cos_c = jnp.concatenate([jnp.cos(ar), jnp.cos(ar), jnp.cos(ac), jnp.cos(ac)], axis=1)
    sin_c = jnp.concatenate([-jnp.sin(ar), jnp.sin(ar), -jnp.sin(ac), jnp.sin(ac)], axis=1)
    cos_t = jnp.concatenate([cos_c, cos_c], axis=1)
    sin_t = jnp.concatenate([sin_c, sin_c], axis=1)
    cos_t = jnp.concatenate([cos_t, jnp.ones((tm, LANES), F32)], axis=0)
    sin_t = jnp.concatenate([sin_t, jnp.zeros((tm, LANES), F32)], axis=0)
    return cos_t, sin_t


def _rope_lanes(x, cos, sin):
    nf = DIFF_HEAD_DIM // 4
    lane = lax.broadcasted_iota(jnp.int32, x.shape, 1)
    partner = jnp.where((lane % (2 * nf)) < nf,
                        pltpu.roll(x, LANES - nf, axis=1), pltpu.roll(x, nf, axis=1))
    return x * cos + partner * sin


def _diff_qkv_epilogue(D, y, extra, outs):
    cos, sin = extra[0][...], extra[1][...]
    q_ref, k_ref, v_ref = outs
    qscale = DIFF_HEAD_DIM ** -0.5
    for h in range(DIFF_HEADS):
        sl = slice(h * LANES, (h + 1) * LANES)
        q_ref[h] = (_rope_lanes(y[:, sl], cos, sin) * qscale).astype(BF16)
        k_ref[h] = _rope_lanes(y[:, D + h * LANES:D + (h + 1) * LANES], cos, sin).astype(BF16)
        v_ref[h] = y[:, 2 * D + h * LANES:2 * D + (h + 1) * LANES].astype(BF16)


def _diff_attn_kernel(lam_init, n_seg, lam_ref, subg_ref, q_ref, *refs):
    kv = refs[:2 * n_seg]
    o_ref = refs[2 * n_seg]
    lam = lam_ref[...]
    lam_full = (jnp.exp(jnp.sum(lam[0:1] * lam[1:2], axis=-1, keepdims=True))
                - jnp.exp(jnp.sum(lam[2:3] * lam[3:4], axis=-1, keepdims=True)) + lam_init)
    q = q_ref[0]
    lane = lax.broadcasted_iota(jnp.int32, q.shape, 1)
    zero = jnp.zeros_like(q)
    probs = []
    for c in range(2):
        qc = jnp.where((lane >= c * DIFF_HEAD_DIM) & (lane < (c + 1) * DIFF_HEAD_DIM), q, zero)
        s = [_dot_nt(qc, kv[2 * j][0]) for j in range(n_seg)]
        m = functools.reduce(jnp.maximum, [jnp.max(t, axis=-1, keepdims=True) for t in s])
        e = [jnp.exp(t - m) for t in s]
        inv = 1.0 / functools.reduce(jnp.add, [jnp.sum(t, axis=-1, keepdims=True) for t in e])
        probs.append([t * inv for t in e])
    o = None
    for j in range(n_seg):
        a = (probs[0][j] - lam_full * probs[1][j]).astype(BF16)
        t = _dot(a, kv[2 * j + 1][0])
        o = t if o is None else o + t
    o = o * lax.rsqrt(jnp.mean(o * o, axis=-1, keepdims=True) + EPS)
    o_ref[0] = o * subg_ref[...] * (1.0 - lam_init)


def _diff_attention(dims, q, k, v, lam, subg, lam_init, prev, latent, tq):
    H = DIFF_HEADS
    B, L, C = dims.B, dims.L, dims.C
    cblk0 = dims.TL // C
    if latent:
        nq = L // tq
        q_map = lambda b, h, i: (h, b * nq + i, 0)
        kv_specs = [
            pl.BlockSpec((1, L, LANES), lambda b, h, i: (h, b, 0)),
            pl.BlockSpec((1, L, LANES), lambda b, h, i: (h, b, 0)),
            pl.BlockSpec((1, C, LANES), lambda b, h, i: (h, cblk0 + b, 0)),
            pl.BlockSpec((1, C, LANES), lambda b, h, i: (h, cblk0 + b, 0)),
        ]
        kv_args = [k, v, k, v]
    else:
        nq = C // tq
        qblk0 = dims.TL // tq
        q_map = lambda b, h, i: (h, qblk0 + b * nq + i, 0)
        kv_specs = [
            pl.BlockSpec((1, C, LANES), lambda b, h, i: (h, cblk0 + b, 0)),
            pl.BlockSpec((1, C, LANES), lambda b, h, i: (h, cblk0 + b, 0)),
        ]
        kv_args = [k, v]
    n_seg = len(kv_args) // 2
    in_specs = [
        pl.BlockSpec(lam.shape, lambda b, h, i: (0, 0)),
        pl.BlockSpec((1, LANES), lambda b, h, i: (0, 0)),
        pl.BlockSpec((1, tq, LANES), q_map),
        *kv_specs,
    ]
    args = [lam, subg.reshape(1, LANES), q, *kv_args]
    kern = functools.partial(_diff_attn_kernel, lam_init, n_seg)
    aliases = {}
    if prev is not None:
        in_specs.append(pl.BlockSpec(memory_space=pl.ANY))
        args.append(prev)
        aliases = {len(args) - 1: 0}
        base = kern
        kern = lambda *r: base(*r[:-2], r[-1])
    return pl.pallas_call(
        kern,
        grid=(B, H, nq),
        in_specs=in_specs,
        out_specs=pl.BlockSpec((1, tq, LANES), q_map),
        out_shape=jax.ShapeDtypeStruct((H, dims.T, LANES), F32),
        input_output_aliases=aliases,
        compiler_params=_cparams("parallel", "parallel", "parallel"),
        name="diff_attn",
    )(*args)


def _heads_to_lanes(ref):
    return jnp.concatenate([ref[h] for h in range(ref.shape[0])], axis=1)


def _diff_layer(dims, X, g, tabs, w_in, lam, subg, w_out, depth_idx):
    D, H = dims.D, DIFF_HEADS
    tm = 256
    cos_t, sin_t = _rope_tables(dims, tm)
    n_lat, bpb = dims.TL // tm, dims.L // tm
    rope_spec = pl.BlockSpec((tm, LANES), lambda i: (jnp.where(i < n_lat, i % bpb, bpb), 0))
    hm = jax.ShapeDtypeStruct((H, dims.T, LANES), BF16)
    hm_spec = pl.BlockSpec((H, tm, LANES), lambda i: (0, i, 0))
    q, k, v = _norm_mod_proj(
        dims, X, g, tabs[0], tabs[1], w_in.astype(BF16), functools.partial(_diff_qkv_epilogue, D),
        (hm, hm, hm), (hm_spec, hm_spec, hm_spec),
        extra=(cos_t, sin_t), extra_specs=(rope_spec, rope_spec), tm=tm, name="diff_in")
    lam_init = 0.8 - 0.6 * math.exp(-0.3 * depth_idx)
    o = _diff_attention(dims, q, k, v, lam, subg, lam_init, None, True, min(256, dims.L))
    o = _diff_attention(dims, q, k, v, lam, subg, lam_init, o, False, min(256, dims.C))
    return _out_proj(dims, dims.T, [o], [pl.BlockSpec((H, 256, LANES), lambda i: (0, i, 0))],
                     _heads_to_lanes, w_out.astype(BF16), tabs[2], X, name="diff_out")


def _conv_silu_kernel(width, x_ref, w_ref, b_ref, o_ref):
    x = x_ref[...]
    n = x.shape[0]
    row = lax.broadcasted_iota(jnp.int32, x.shape, 0)
    pad = (width - 1) // 2
    acc = jnp.zeros_like(x) + b_ref[...]
    for t in range(width):
        off = t - pad
        xs = x if off == 0 else pltpu.roll(x, (-off) % n, axis=0)
        ok = (row + off >= 0) & (row + off < n)
        acc = acc + jnp.where(ok, xs, 0.0) * w_ref[t:t + 1, :]
    o_ref[...] = _silu(acc)


def _conv_silu(dims, xbc, cw, cb, prev, seq_len, row0):
    n_ch = xbc.shape[1]
    tc = 512
    seq0 = row0 // seq_len
    width = cw.shape[0]
    in_specs = [
        pl.BlockSpec((seq_len, tc), lambda b, j: (seq0 + b, j)),
        pl.BlockSpec((width, tc), lambda b, j: (0, j)),
        pl.BlockSpec((1, tc), lambda b, j: (0, j)),
    ]
    args = [xbc, cw, cb.reshape(1, n_ch)]
    kern = functools.partial(_conv_silu_kernel, width)
    aliases = {}
    if prev is not None:
        in_specs.append(pl.BlockSpec(memory_space=pl.ANY))
        args.append(prev)
        aliases = {3: 0}
        kern = lambda x, w, b, _p, o: _conv_silu_kernel(width, x, w, b, o)
    return pl.pallas_call(
        kern,
        grid=(dims.B, n_ch // tc),
        in_specs=in_specs,
        out_specs=pl.BlockSpec((seq_len, tc), lambda b, j: (seq0 + b, j)),
        out_shape=jax.ShapeDtypeStruct(xbc.shape, F32),
        input_output_aliases=aliases,
        compiler_params=_cparams("parallel", "parallel"),
        name="ssd_conv",
    )(*args)


def _tri(n, reverse):
    r = lax.broadcasted_iota(jnp.int32, (n, n), 0)
    c = lax.broadcasted_iota(jnp.int32, (n, n), 1)
    return (c >= r) if reverse else (c <= r)


def _ssd_scan_kernel(nc, n_heads, a_ref, bias_ref, expand_ref,
                     xs_f, bm_f, cm_f, dt_f, xs_b, bm_b, cm_b, dt_b, s0_ref,
                     of_ref, ob_ref, sout_ref, st_ref):
    s = pl.program_id(1)

    @pl.when(s == 0)
    def _():
        st_ref[...] = s0_ref[0]

    G = SSD_GROUPS
    N = SSD_STATE
    P = SSD_HEAD_DIM
    hpg = n_heads // G
    expand = expand_ref[...]
    for d, (xs_ref, bm_ref, cm_ref, dt_ref, o_ref) in enumerate(
            ((xs_f, bm_f, cm_f, dt_f, of_ref), (xs_b, bm_b, cm_b, dt_b, ob_ref))):
        rev = d == 1
        Q = xs_ref.shape[0]
        mask = _tri(Q, rev)
        tri = jnp.where(mask, 1.0, 0.0).astype(BF16)
        lane = lax.broadcasted_iota(jnp.int32, (Q, LANES), 1)
        mine = (lane >= d * n_heads) & (lane < (d + 1) * n_heads)
        dt = jnp.where(mine, _softplus(dt_ref[...] + bias_ref[...]), 0.0)
        if d == 1:
            dt = pltpu.roll(dt, LANES - n_heads, axis=1)
        la = dt * a_ref[d:d + 1, :]
        cs = _ones_dot(tri, la)
        tot = cs[Q - 1:Q, :] if not rev else cs[0:1, :]
        csT = cs.T
        dtT = dt.T
        ecs = jnp.exp(cs)
        wend = dt * jnp.exp(tot - cs)
        ecs_x = _dot(ecs.astype(BF16), expand) + _dot((ecs - ecs.astype(BF16).astype(F32)).astype(BF16), expand)
        wend_x = _dot(wend.astype(BF16), expand) + _dot((wend - wend.astype(BF16).astype(F32)).astype(BF16), expand)
        etot = jnp.exp(tot)
        etot_x = _dot(etot.astype(BF16), expand) + _dot((etot - etot.astype(BF16).astype(F32)).astype(BF16), expand)
        xs = xs_ref[...]
        xw = (xs * wend_x).astype(BF16)
        xs_bf = xs.astype(BF16)
        outs = []
        for g in range(G):
            cg = cm_ref[:, g * N:(g + 1) * N].astype(BF16)
            bg = bm_ref[:, g * N:(g + 1) * N].astype(BF16)
            cb = _dot_nt(cg, bg)
            gs = slice(g * hpg * P, (g + 1) * hpg * P)
            st = st_ref[d, g]
            inter = _dot(cg, st.astype(BF16)) * ecs_x[:, gs]
            intra = []
            for hh in range(hpg):
                h = g * hpg + hh
                seg = cs[:, h:h + 1] - csT[h:h + 1, :]
                m = cb * jnp.exp(jnp.where(mask, seg, -jnp.inf)) * dtT[h:h + 1, :]
                intra.append(_dot(m.astype(BF16), xs_bf[:, h * P:(h + 1) * P]))
            outs.append(inter + jnp.concatenate(intra, axis=1))
            st_ref[d, g] = st * etot_x[:, gs] + _dot_tn(bg, xw[:, gs])
        o_ref[...] = jnp.concatenate(outs, axis=1)

    @pl.when(s == nc - 1)
    def _():
        sout_ref[0] = st_ref[...]


def _ssd_scan(dims, xbc_act, dt, a, bias, s0, prev, seq_len, row0, q):
    d_inner = xbc_act.shape[1] - 2 * SSD_GROUPS * SSD_STATE
    n_heads = d_inner // SSD_HEAD_DIM
    G, N = SSD_GROUPS, SSD_STATE
    gn = G * N
    nc = seq_len // q
    blk0 = row0 // q
    fwd = lambda b, s: blk0 + b * nc + s
    bwd = lambda b, s: blk0 + b * nc + (nc - 1 - s)
    xs_c, bm_c, cm_c = 0, d_inner // gn, d_inner // gn + 1

    def stream(pos):
        return [
            pl.BlockSpec((q, d_inner), lambda b, s: (pos(b, s), xs_c)),
            pl.BlockSpec((q, gn), lambda b, s: (pos(b, s), bm_c)),
            pl.BlockSpec((q, gn), lambda b, s: (pos(b, s), cm_c)),
            pl.BlockSpec((q, LANES), lambda b, s: (pos(b, s), 0)),
        ]

    expand = (jnp.arange(LANES)[:, None] == (jnp.arange(d_inner)[None, :] // SSD_HEAD_DIM)).astype(BF16)
    a_pad = jnp.zeros((2, LANES), F32).at[:, :n_heads].set(a)
    bias_pad = jnp.zeros((1, LANES), F32).at[0, :2 * n_heads].set(bias.reshape(-1))
    st_shape = (2, G, N, d_inner // G)
    in_specs = [
        pl.BlockSpec((2, LANES), lambda b, s: (0, 0)),
        pl.BlockSpec((1, LANES), lambda b, s: (0, 0)),
        pl.BlockSpec(expand.shape, lambda b, s: (0, 0)),
        *stream(fwd), *stream(bwd),
        pl.BlockSpec((1,) + st_shape, lambda b, s: (b, 0, 0, 0, 0)),
    ]
    args = [a_pad, bias_pad, expand, xbc_act, xbc_act, xbc_act, dt, xbc_act, xbc_act, xbc_act, dt, s0]
    kern = functools.partial(_ssd_scan_kernel, nc, n_heads)
    aliases = {}
    if prev is not None:
        in_specs += [pl.BlockSpec(memory_space=pl.ANY)] * 2
        args += list(prev)
        aliases = {len(args) - 2: 0, len(args) - 1: 1}
        base = kern
        kern = lambda *r: base(*r[:12], *r[14:])
    o_shape = jax.ShapeDtypeStruct((dims.T, d_inner), F32)
    return pl.pallas_call(
        kern,
        grid=(dims.B, nc),
        in_specs=in_specs,
        out_specs=(
            pl.BlockSpec((q, d_inner), lambda b, s: (fwd(b, s), 0)),
            pl.BlockSpec((q, d_inner), lambda b, s: (bwd(b, s), 0)),
            pl.BlockSpec((1,) + st_shape, lambda b, s: (b, 0, 0, 0, 0)),
        ),
        out_shape=(o_shape, o_shape, jax.ShapeDtypeStruct((dims.B,) + st_shape, F32)),
        scratch_shapes=[pltpu.VMEM(st_shape, F32)],
        input_output_aliases=aliases,
        compiler_params=_cparams("parallel", "arbitrary"),
        name="ssd_scan",
    )(*args)


def _ssd_finish(group, of_ref, ob_ref, xs_ref, z_ref, dskip_ref, ng_ref):
    y = of_ref[...] + ob_ref[...] + xs_ref[...] * dskip_ref[...]
    return _group_rmsnorm(y * _silu(z_ref[...]), ng_ref[...], group)


def _ssd_layer(dims, X, g, tabs, w_in, conv_w, conv_b, a_log, dt_bias, d_skip, norm_g, w_out):
    D = dims.D
    d_inner = w_out.shape[0]
    n_heads = d_inner // SSD_HEAD_DIM
    conv_dim = conv_w.shape[1]
    n_dt = 2 * n_heads
    w_pad = jnp.concatenate([w_in, jnp.zeros((D, LANES - n_dt), w_in.dtype)], axis=1).astype(BF16)

    def epi(y, extra, outs):
        outs[0][...] = y[:, :d_inner]
        outs[1][...] = y[:, d_inner:d_inner + conv_dim]
        outs[2][...] = y[:, d_inner + conv_dim:]

    tm = 256
    row = lambda i: (i, 0)
    z, xbc, dt = _norm_mod_proj(
        dims, X, g, tabs[0], tabs[1], w_pad, epi,
        (jax.ShapeDtypeStruct((dims.T, d_inner), F32), jax.ShapeDtypeStruct((dims.T, conv_dim), F32),
         jax.ShapeDtypeStruct((dims.T, LANES), F32)),
        (pl.BlockSpec((tm, d_inner), row), pl.BlockSpec((tm, conv_dim), row), pl.BlockSpec((tm, LANES), row)),
        tm=tm, name="ssd_in")
    act = _conv_silu(dims, xbc, conv_w, conv_b, None, dims.L, 0)
    act = _conv_silu(dims, xbc, conv_w, conv_b, act, dims.C, dims.TL)
    a = -jnp.exp(a_log.astype(F32))
    q = min(128, dims.C)
    st_shape = (dims.B, 2, SSD_GROUPS, SSD_STATE, d_inner // SSD_GROUPS)
    oc_f, oc_b, s_ctx = _ssd_scan(dims, act, dt, a, dt_bias, jnp.zeros(st_shape, F32), None, dims.C, dims.TL, q)
    o_f, o_b, _ = _ssd_scan(dims, act, dt, a, dt_bias, s_ctx, (oc_f, oc_b), dims.L, 0, q)
    dskip_x = jnp.repeat(d_skip.astype(F32), SSD_HEAD_DIM).reshape(1, d_inner)
    wide = pl.BlockSpec((tm, d_inner), row)
    vec = pl.BlockSpec((1, d_inner), lambda i: (0, 0))
    return _out_proj(
        dims, dims.T, [o_f, o_b, act, z, dskip_x, norm_g.reshape(1, d_inner)],
        [wide, wide, wide, wide, vec, vec],
        functools.partial(_ssd_finish, d_inner // SSD_GROUPS), w_out.astype(BF16), tabs[2], X, tm=tm,
        name="ssd_out")


def _hgrn_scan_kernel(nc, layer_idx, sub, lb_ref, q_f, f_f, v_f, q_b, f_b, v_b, s0_ref,
                      of_ref, ob_ref, sout_ref, st_ref):
    s = pl.program_id(1)

    @pl.when(s == 0)
    def _():
        st_ref[...] = s0_ref[0]

    lbr = lb_ref[...]
    e = jnp.exp(lbr - jnp.max(lbr, axis=0, keepdims=True))
    soft = e / jnp.sum(e, axis=0, keepdims=True)
    lb = jnp.zeros_like(soft[0:1])
    for l in range(1, layer_idx + 1):
        lb = lb + soft[l:l + 1]

    K = HGRN_HEAD_DIM
    n_heads = q_f.shape[1] // K
    for d, (q_ref, f_ref, v_ref, o_ref) in enumerate(((q_f, f_f, v_f, of_ref), (q_b, f_b, v_b, ob_ref))):
        rev = d == 1
        Q = q_ref.shape[0]
        nsub = Q // sub
        r = lax.broadcasted_iota(jnp.int32, (Q, Q), 0)
        c = lax.broadcasted_iota(jnp.int32, (Q, Q), 1)
        mask = (c >= r) if rev else (c <= r)
        same = (r // sub) == (c // sub)
        tri = jnp.where(mask, 1.0, 0.0).astype(BF16)
        tri_blk = jnp.where(mask & same, 1.0, 0.0).astype(BF16)
        f = lb + (1.0 - lb) * _sigmoid(f_ref[...])
        kk = 1.0 - f
        lg = jnp.log(f)
        cs = _ones_dot(tri, lg)
        a = _ones_dot(tri_blk, lg)
        rr = cs - a
        tot = cs[Q - 1:Q, :] if not rev else cs[0:1, :]
        qv = q_ref[...]
        qt = (qv * jnp.exp(a)).astype(BF16)
        qe = (qv * jnp.exp(cs)).astype(BF16)
        kend = (kk * jnp.exp(tot - cs)).astype(BF16)
        etot = jnp.exp(tot)
        v_bf = v_ref[...].astype(BF16)
        outs = []
        for h in range(n_heads):
            hs = slice(h * K, (h + 1) * K)
            w_rows = []
            for i in range(nsub):
                r_i = rr[i * sub:i * sub + 1, hs]
                kt = (kk[:, hs] * jnp.exp(jnp.minimum(r_i - cs[:, hs], EXP_CLAMP))).astype(BF16)
                w_rows.append(_dot_nt(qt[i * sub:(i + 1) * sub, hs], kt))
            w = jnp.where(mask, jnp.concatenate(w_rows, axis=0), 0.0).astype(BF16)
            st = st_ref[d, h]
            outs.append(_dot(w, v_bf[:, hs]) + _dot_nt(qe[:, hs], st.astype(BF16)))
            st_ref[d, h] = st * etot[:, hs] + _dot_tn(v_bf[:, hs], kend[:, hs])
        o_ref[...] = jnp.concatenate(outs, axis=1)

    @pl.when(s == nc - 1)
    def _():
        sout_ref[0] = st_ref[...]


def _hgrn_scan(dims, layer_idx, lb_all, qa, ffw, fbw, va, s0, prev, seq_len, row0, q, sub):
    D = dims.D
    n_heads = D // HGRN_HEAD_DIM
    nc = seq_len // q
    blk0 = row0 // q
    fwd = lambda b, s: (blk0 + b * nc + s, 0)
    bwd = lambda b, s: (blk0 + b * nc + (nc - 1 - s), 0)
    st_shape = (2, n_heads, HGRN_HEAD_DIM, HGRN_HEAD_DIM)
    blk = lambda m: pl.BlockSpec((q, D), m)
    in_specs = [
        pl.BlockSpec(lb_all.shape, lambda b, s: (0, 0)),
        blk(fwd), blk(fwd), blk(fwd), blk(bwd), blk(bwd), blk(bwd),
        pl.BlockSpec((1,) + st_shape, lambda b, s: (b, 0, 0, 0, 0)),
    ]
    args = [lb_all, qa, ffw, va, qa, fbw, va, s0]
    kern = functools.partial(_hgrn_scan_kernel, nc, layer_idx, sub)
    aliases = {}
    if prev is not None:
        in_specs += [pl.BlockSpec(memory_space=pl.ANY)] * 2
        args += list(prev)
        aliases = {len(args) - 2: 0, len(args) - 1: 1}
        base = kern
        kern = lambda *r: base(*r[:8], *r[10:])
    o_shape = jax.ShapeDtypeStruct((dims.T, D), F32)
    return pl.pallas_call(
        kern,
        grid=(dims.B, nc),
        in_specs=in_specs,
        out_specs=(blk(fwd), blk(bwd), pl.BlockSpec((1,) + st_shape, lambda b, s: (b, 0, 0, 0, 0))),
        out_shape=(o_shape, o_shape, jax.ShapeDtypeStruct((dims.B,) + st_shape, F32)),
        scratch_shapes=[pltpu.VMEM(st_shape, F32)],
        input_output_aliases=aliases,
        compiler_params=_cparams("parallel", "arbitrary"),
        name="hgrn_scan",
    )(*args)


def _hgrn_finish(of_ref, ob_ref, g_ref, ng_ref):
    o = _group_rmsnorm(of_ref[...] + ob_ref[...], ng_ref[...], HGRN_HEAD_DIM)
    return o * _silu(g_ref[...])


def _hgrn_layer(dims, X, g, tabs, w_in, lb_all, layer_idx, norm_g, w_out, ctx_out):
    D = dims.D

    def epi(y, extra, outs):
        outs[0][...] = _silu(y[:, :D])
        for j in range(1, 5):
            outs[j][...] = y[:, j * D:(j + 1) * D]

    tm = 256
    row = lambda i: (i, 0)
    shp = jax.ShapeDtypeStruct((dims.T, D), F32)
    spec = pl.BlockSpec((tm, D), row)
    qa, ffw, fbw, va, ga = _norm_mod_proj(
        dims, X, g, tabs[0], tabs[1], w_in.astype(BF16), epi, (shp,) * 5, (spec,) * 5, tm=tm, name="hgrn_in")
    q, sub = min(64, dims.C), 16
    n_heads = D // HGRN_HEAD_DIM
    s0 = jnp.zeros((dims.B, 2, n_heads, HGRN_HEAD_DIM, HGRN_HEAD_DIM), F32)
    oc_f, oc_b, s_ctx = _hgrn_scan(dims, layer_idx, lb_all, qa, ffw, fbw, va, s0, None, dims.C, dims.TL, q, sub)
    o_f, o_b, _ = _hgrn_scan(dims, layer_idx, lb_all, qa, ffw, fbw, va, s_ctx, (oc_f, oc_b), dims.L, 0, q, sub)
    n_rows = dims.T if ctx_out else dims.TL
    return _out_proj(dims, n_rows, [o_f, o_b, ga, norm_g.reshape(1, D)],
                     [spec, spec, spec, pl.BlockSpec((1, D), lambda i: (0, 0))],
                     _hgrn_finish, w_out.astype(BF16), tabs[2], X, tm=tm, name="hgrn_out")


def _topk_rows(s, k):
    n = s.shape[0]
    rows = lax.broadcasted_iota(jnp.int32, s.shape, 0)
    vals = []
    for _ in range(k):
        m = jnp.max(s, axis=0, keepdims=True)
        vals.append(m)
        first = jnp.min(jnp.where(s == m, rows, n), axis=0, keepdims=True)
        s = jnp.where(rows == first, -jnp.inf, s)
    return jnp.concatenate(vals, axis=0)


def _peer_pre_kernel(n_heads, x_ref, g_ref, sh_ref, sc_ref, wq_ref, keys_ref,
                     h_ref, s1_ref, s2_ref, stat_ref):
    h = _norm_mod(x_ref[...], g_ref[...], sh_ref[0], sc_ref[0]).astype(BF16)
    h_ref[...] = h
    q = _dot(h, wq_ref[...])
    dk = keys_ref.shape[2]
    taus, m1s, m2s, zs = [], [], [], []
    for hh in range(n_heads):
        sv = []
        for c, s_ref in enumerate((s1_ref, s2_ref)):
            j = hh * 2 + c
            st = _dot_nt(keys_ref[j], q[:, j * dk:(j + 1) * dk].astype(BF16))
            s_ref[hh] = st
            sv.append(_topk_rows(st, PEER_TOPK))
        cand = jnp.concatenate([sv[0][a:a + 1] + sv[1] for a in range(PEER_TOPK)], axis=0)
        best = _topk_rows(cand, PEER_TOPK)
        taus.append(best[PEER_TOPK - 1:PEER_TOPK])
        m1s.append(sv[0][0:1])
        m2s.append(sv[1][0:1])
        zs.append(1.0 / jnp.sum(jnp.exp(best - best[0:1]), axis=0, keepdims=True))
    for j, rows in enumerate((taus, m1s, m2s, zs)):
        stat_ref[j] = jnp.concatenate(rows, axis=0)


def _peer_pre(dims, X, n_rows, g, sh_tab, sc_tab, wq, keys, tb):
    D = dims.D
    H, _, n_keys, dk = keys.shape
    keys2 = keys.reshape(H * 2, n_keys, dk).astype(BF16)
    tab_spec = pl.BlockSpec((1, 1, D), lambda i: (dims.mod_row(i, tb), 0, 0))
    s_shape = jax.ShapeDtypeStruct((H, n_keys, n_rows), F32)
    s_spec = pl.BlockSpec((H, n_keys, tb), lambda i: (0, 0, i))
    return pl.pallas_call(
        functools.partial(_peer_pre_kernel, H),
        grid=(n_rows // tb,),
        in_specs=[
            pl.BlockSpec((tb, D), lambda i: (i, 0)),
            pl.BlockSpec((1, D), lambda i: (0, 0)),
            tab_spec, tab_spec,
            pl.BlockSpec(wq.shape, lambda i: (0, 0)),
            pl.BlockSpec(keys2.shape, lambda i: (0, 0, 0)),
        ],
        out_specs=(
            pl.BlockSpec((tb, D), lambda i: (i, 0)),
            s_spec, s_spec,
            pl.BlockSpec((4, H, tb), lambda i: (0, 0, i)),
        ),
        out_shape=(jax.ShapeDtypeStruct((n_rows, D), BF16), s_shape, s_shape,
                   jax.ShapeDtypeStruct((4, H, n_rows), F32)),
        compiler_params=_cparams("parallel"),
        name="peer_pre",
    )(X, g.reshape(1, D), sh_tab, sc_tab, wq.astype(BF16), keys2)


def _peer_main_kernel(n_chunks, final, h_ref, s1_ref, s2_ref, stat_ref, u_ref, vt_ref, x_ref, gate_ref,
                      fg_ref, o_ref, p1_ref, p2_ref, acc_ref):
    e = pl.program_id(1)
    n_heads, n_keys, _ = s1_ref.shape

    @pl.when(e == 0)
    def _():
        for hh in range(n_heads):
            p1_ref[hh] = jnp.exp(s1_ref[hh] - stat_ref[1, hh:hh + 1, :]) * stat_ref[3, hh:hh + 1, :]
            p2_ref[hh] = jnp.exp(s2_ref[hh] - stat_ref[2, hh:hh + 1, :])
        acc_ref[...] = jnp.zeros_like(acc_ref)

    act = _gelu_exact(_dot_nt(u_ref[...], h_ref[...]))
    per_chunk = u_ref.shape[0] // n_keys
    parts = []
    for ii in range(per_chunk):
        i1 = e * per_chunk + ii
        gsum = None
        for hh in range(n_heads):
            c = s1_ref[hh, pl.ds(i1, 1), :] + s2_ref[hh]
            gh = jnp.where(c >= stat_ref[0, hh:hh + 1, :], p1_ref[hh, pl.ds(i1, 1), :] * p2_ref[hh], 0.0)
            gsum = gh if gsum is None else gsum + gh
        parts.append(gsum * act[ii * n_keys:(ii + 1) * n_keys])
    p = jnp.concatenate(parts, axis=0).astype(BF16)
    acc_ref[...] += _dot(vt_ref[...], p)

    @pl.when(e == n_chunks - 1)
    def _():
        out = x_ref[...] + gate_ref[0] * acc_ref[...].T
        if final:
            out = out * lax.rsqrt(jnp.mean(out * out, axis=-1, keepdims=True) + EPS) * fg_ref[...]
        o_ref[...] = out


def _peer_layer(dims, X, n_rows, g, tabs, wq, keys, u, v, final_g):
    D = dims.D
    H, _, n_keys, _ = keys.shape
    n_exp = u.shape[0]
    h2, s1, s2, stat = _peer_pre(dims, X, n_rows, g, tabs[3], tabs[4], wq, keys, dims.tile(n_rows, 256))
    tb = dims.tile(n_rows, 512)
    ec = max(256, n_keys)
    n_chunks = n_exp // ec
    u_bf = u.astype(BF16)
    vt_bf = v.astype(BF16).T
    final = final_g is not None
    fg = (final_g if final else jnp.ones((D,), F32)).reshape(1, D)
    s_spec = pl.BlockSpec((H, n_keys, tb), lambda i, e: (0, 0, i))
    return pl.pallas_call(
        functools.partial(_peer_main_kernel, n_chunks, final),
        grid=(n_rows // tb, n_chunks),
        in_specs=[
            pl.BlockSpec((tb, D), lambda i, e: (i, 0)),
            s_spec, s_spec,
            pl.BlockSpec((4, H, tb), lambda i, e: (0, 0, i)),
            pl.BlockSpec((ec, D), lambda i, e: (e, 0)),
            pl.BlockSpec((D, ec), lambda i, e: (0, e)),
            pl.BlockSpec((tb, D), lambda i, e: (i, 0)),
            pl.BlockSpec((1, 1, D), lambda i, e: (dims.mod_row(i, tb), 0, 0)),
            pl.BlockSpec((1, D), lambda i, e: (0, 0)),
        ],
        out_specs=pl.BlockSpec((tb, D), lambda i, e: (i, 0)),
        out_shape=jax.ShapeDtypeStruct((n_rows, D), F32),
        scratch_shapes=[pltpu.VMEM((H, n_keys, tb), F32), pltpu.VMEM((H, n_keys, tb), F32),
                        pltpu.VMEM((D, tb), F32)],
        compiler_params=_cparams("parallel", "arbitrary"),
        name="peer_main",
    )(h2, s1, s2, stat, u_bf, vt_bf, X, tabs[5], fg)


def kernel(x, c, ctx, c_ctx, w_mod, b_mod, norm_g, four_w_out, diff_w_in, diff_lam, diff_norm_g, diff_w_out,
           ssd_w_in, ssd_conv_w, ssd_conv_b, ssd_a_log, ssd_dt_bias, ssd_d, ssd_norm_g, ssd_w_out, hgrn_w_in,
           hgrn_lb, hgrn_norm_g, hgrn_w_out, peer_wq, peer_keys, peer_u, peer_v, final_g):
    B, L, D = x.shape
    C = ctx.shape[1]
    depth = w_mod.shape[0]
    n_mixers = 4
    dims = Dims(B, L, C, D)
    assert B + 1 <= MOD_ROWS

    X = jnp.concatenate([x.reshape(B * L, D), ctx.reshape(B * C, D)], axis=0)
    cc = jnp.concatenate([c, c_ctx[None, :], jnp.zeros((MOD_ROWS - B - 1, D), c.dtype)], axis=0)
    mods = _mod_tables(cc, w_mod, b_mod)

    for i in range(depth):
        last = i == depth - 1
        kind, j = i % n_mixers, i // n_mixers
        tabs = [mods[i, :, k * D:(k + 1) * D].reshape(MOD_ROWS, 1, D) for k in range(6)]
        g1, g2 = norm_g[i, 0], norm_g[i, 1]
        if kind == 0:
            X = _fourier_layer(dims, X, g1, tabs, four_w_out[j])
        elif kind == 1:
            X = _diff_layer(dims, X, g1, tabs, diff_w_in[j], diff_lam[j], diff_norm_g[j], diff_w_out[j], i)
        elif kind == 2:
            X = _ssd_layer(dims, X, g1, tabs, ssd_w_in[j], ssd_conv_w[j], ssd_conv_b[j], ssd_a_log[j],
                           ssd_dt_bias[j], ssd_d[j], ssd_norm_g[j], ssd_w_out[j])
        else:
            X = _hgrn_layer(dims, X, g1, tabs, hgrn_w_in[j], hgrn_lb, i, hgrn_norm_g[j], hgrn_w_out[j],
                            not last)
        n_rows = dims.TL if last else dims.T
        X = _peer_layer(dims, X, n_rows, g2, tabs, peer_wq[i], peer_keys[i], peer_u[i], peer_v[i],
                        final_g if last else None)
    return X[:B * L].reshape(B, L, D)
```

```python
import functools
import math

import jax
import jax.numpy as jnp
import numpy as np
from jax import lax
from jax.experimental import pallas as pl
from jax.experimental.pallas import tpu as pltpu

F32 = jnp.float32
BF16 = jnp.bfloat16

EPS = 1e-6
ROPE_BASE = 10000.0
GRID_W = 64
FOURIER_GROUPS = 8
DIFF_HEADS = 8
DIFF_HEAD_DIM = 64
SSD_HEAD_DIM = 64
SSD_STATE = 128
SSD_GROUPS = 8
HGRN_HEAD_DIM = 128
PEER_TOPK = 16

LANES = 128
VMEM_LIMIT_BYTES = 56 * 2**20
MOD_ROWS = 16
EXP_CLAMP = 80.0


def _cparams(*sem):
    return pltpu.CompilerParams(dimension_semantics=sem, vmem_limit_bytes=VMEM_LIMIT_BYTES)


class Dims:
    def __init__(self, B, L, C, D):
        self.B, self.L, self.C, self.D = B, L, C, D
        self.TL = B * L
        self.T = B * L + B * C

    def tile(self, n_rows, pref):
        t = pref
        while self.L % t or n_rows % t:
            t //= 2
        return t

    def mod_row(self, i, tm):
        n_lat = self.TL // tm
        return jnp.where(i < n_lat, i // (self.L // tm), self.B)


def _silu(x):
    return x * (1.0 / (1.0 + jnp.exp(-x)))


def _sigmoid(x):
    return 1.0 / (1.0 + jnp.exp(-x))


def _softplus(x):
    return jnp.maximum(x, 0.0) + jnp.log(1.0 + jnp.exp(-jnp.abs(x)))


def _gelu_exact(x):
    return 0.5 * x * (1.0 + lax.erf(x * (1.0 / math.sqrt(2.0))))


def _norm_mod(x, g, shift, scale):
    y = x * lax.rsqrt(jnp.mean(x * x, axis=-1, keepdims=True) + EPS)
    return (y * g) * (1.0 + scale) + shift


def _group_rmsnorm(y, g, group):
    parts = []
    for s in range(0, y.shape[1], group):
        p = y[:, s:s + group]
        parts.append(p * lax.rsqrt(jnp.mean(p * p, axis=-1, keepdims=True) + EPS))
    return jnp.concatenate(parts, axis=1) * g


def _split3(x):
    a = x.astype(BF16)
    r = x - a.astype(F32)
    b = r.astype(BF16)
    c = (r - b.astype(F32)).astype(BF16)
    return a, b, c


def _dot(a, b):
    return jnp.dot(a, b, preferred_element_type=F32)


def _dot_nt(a, b):
    return lax.dot_general(a, b, (((1,), (1,)), ((), ())), preferred_element_type=F32)


def _dot_tn(a, b):
    return lax.dot_general(a, b, (((0,), (0,)), ((), ())), preferred_element_type=F32)


def _ones_dot(mat01, x):
    a, b, c = _split3(x)
    return _dot(mat01, a) + _dot(mat01, b) + _dot(mat01, c)


def _mod_kernel(c_ref, w_ref, b_ref, o_ref):
    h = _silu(c_ref[...]).astype(BF16)
    o_ref[0] = _dot(h, w_ref[0].astype(BF16)) + b_ref[0]


def _mod_tables(cc, w_mod, b_mod):
    depth, D, N = w_mod.shape
    tn = 1536
    return pl.pallas_call(
        _mod_kernel,
        grid=(depth, N // tn),
        in_specs=[
            pl.BlockSpec((MOD_ROWS, D), lambda l, j: (0, 0)),
            pl.BlockSpec((1, D, tn), lambda l, j: (l, 0, j)),
            pl.BlockSpec((1, 1, tn), lambda l, j: (l, 0, j)),
        ],
        out_specs=pl.BlockSpec((1, MOD_ROWS, tn), lambda l, j: (l, 0, j)),
        out_shape=jax.ShapeDtypeStruct((depth, MOD_ROWS, N), F32),
        compiler_params=_cparams("parallel", "parallel"),
        name="mod_tables",
    )(cc, w_mod, b_mod.reshape(depth, 1, N))


def _proj_kernel(epilogue, n_extra, x_ref, g_ref, sh_ref, sc_ref, w_ref, *rest):
    h = _norm_mod(x_ref[...], g_ref[...], sh_ref[0], sc_ref[0])
    y = _dot(h.astype(BF16), w_ref[...])
    epilogue(y, rest[:n_extra], rest[n_extra:])


def _norm_mod_proj(dims, x, g, shift_tab, scale_tab, w, epilogue, out_shapes, out_specs,
                   extra=(), extra_specs=(), tm=256, name="proj"):
    D = dims.D
    n_rows = x.shape[0]
    tab_spec = pl.BlockSpec((1, 1, D), lambda i: (dims.mod_row(i, tm), 0, 0))
    return pl.pallas_call(
        functools.partial(_proj_kernel, epilogue, len(extra)),
        grid=(n_rows // tm,),
        in_specs=[
            pl.BlockSpec((tm, D), lambda i: (i, 0)),
            pl.BlockSpec((1, D), lambda i: (0, 0)),
            tab_spec, tab_spec,
            pl.BlockSpec(w.shape, lambda i: (0, 0)),
            *extra_specs,
        ],
        out_specs=out_specs,
        out_shape=out_shapes,
        compiler_params=_cparams("parallel"),
        name=name,
    )(x, g.reshape(1, D), shift_tab, scale_tab, w, *extra)


def _out_proj_kernel(prologue, n_in, *refs):
    ins = refs[:n_in]
    w_ref, gate_ref, x_ref, o_ref = refs[n_in:]
    y = prologue(*ins)
    o_ref[...] = x_ref[...] + gate_ref[0] * _dot(y.astype(BF16), w_ref[...])


def _out_proj(dims, n_rows, ins, in_specs, prologue, w, gate_tab, x, tm=256, name="out_proj"):
    D = dims.D
    return pl.pallas_call(
        functools.partial(_out_proj_kernel, prologue, len(ins)),
        grid=(n_rows // tm,),
        in_specs=[
            *in_specs,
            pl.BlockSpec(w.shape, lambda i: (0, 0)),
            pl.BlockSpec((1, 1, D), lambda i: (dims.mod_row(i, tm), 0, 0)),
            pl.BlockSpec((tm, D), lambda i: (i, 0)),
        ],
        out_specs=pl.BlockSpec((tm, D), lambda i: (i, 0)),
        out_shape=jax.ShapeDtypeStruct((n_rows, D), F32),
        compiler_params=_cparams("parallel"),
        name=name,
    )(*ins, w, gate_tab, x)


def _dft_tables(n):
    k = jnp.arange(n, dtype=jnp.int32)
    ang = (2.0 * math.pi / n) * ((k[:, None] * k[None, :]) % n).astype(F32)
    return jnp.cos(ang), jnp.sin(ang)


def _seq_dft_kernel(scale, c_ref, s_ref, h_ref, o_ref):
    n = h_ref.shape[1] // 2
    acc = _dot(c_ref[...], h_ref[:, :n]) - _dot(s_ref[...], h_ref[:, n:])
    o_ref[...] = acc * scale


def _seq_dft(dims, hcs, prev, seq_len, row0, group_width, tm):
    D = dims.D
    cl, sl = _dft_tables(seq_len)
    nblk = seq_len // tm
    blk0 = row0 // tm
    seq0 = row0 // seq_len
    scale = 1.0 / math.sqrt(seq_len * group_width)
    in_specs = [
        pl.BlockSpec((tm, seq_len), lambda b, i: (i, 0)),
        pl.BlockSpec((tm, seq_len), lambda b, i: (i, 0)),
        pl.BlockSpec((seq_len, 2 * D), lambda b, i: (seq0 + b, 0)),
    ]
    args = [cl.astype(BF16), sl.astype(BF16), hcs]
    aliases = {}
    kern = functools.partial(_seq_dft_kernel, scale)
    if prev is not None:
        in_specs.append(pl.BlockSpec(memory_space=pl.ANY))
        args.append(prev)
        aliases = {3: 0}
        kern = lambda c, s, h, _p, o: _seq_dft_kernel(scale, c, s, h, o)
    return pl.pallas_call(
        kern,
        grid=(dims.B, nblk),
        in_specs=in_specs,
        out_specs=pl.BlockSpec((tm, D), lambda b, i: (blk0 + b * nblk + i, 0)),
        out_shape=jax.ShapeDtypeStruct((dims.T, D), F32),
        input_output_aliases=aliases,
        compiler_params=_cparams("parallel", "parallel"),
        name="seq_dft",
    )(*args)


def _fourier_layer(dims, X, g, tabs, w_out):
    D = dims.D
    gw = D // FOURIER_GROUPS
    cg, sg = _dft_tables(gw)
    eye = jnp.eye(FOURIER_GROUPS, dtype=F32)
    wcs = jnp.concatenate([jnp.kron(eye, cg), jnp.kron(eye, sg)], axis=1).astype(BF16)

    def epi(y, extra, outs):
        outs[0][...] = y.astype(BF16)

    hcs = _norm_mod_proj(
        dims, X, g, tabs[0], tabs[1], wcs, epi,
        jax.ShapeDtypeStruct((dims.T, 2 * D), BF16),
        pl.BlockSpec((256, 2 * D), lambda i: (i, 0)), name="fourier_in")
    f = _seq_dft(dims, hcs, None, dims.L, 0, gw, min(512, dims.L))
    f = _seq_dft(dims, hcs, f, dims.C, dims.TL, gw, min(512, dims.C))
    return _out_proj(dims, dims.T, [f], [pl.BlockSpec((256, D), lambda i: (i, 0))],
                     lambda r: r[...], w_out.astype(BF16), tabs[2], X, name="fourier_out")


def _rope_tables(dims, tm):
    L, Dh = dims.L, DIFF_HEAD_DIM
    nf = Dh // 4
    pos = jnp.arange(L)
    row = (pos // GRID_W).astype(F32)
    col = (pos % GRID_W).astype(F32)
    freqs = ROPE_BASE ** (-jnp.arange(nf, dtype=F32) / nf)
    ar, ac = row[:, None] * freqs, col[:, None] * freqs
    cos_c = jnp.concatenate([jnp.cos(ar), jnp.cos(ar), jnp.cos(ac), jnp.cos(ac)], axis=1)
    sin_c = jnp.concatenate([-jnp.sin(ar), jnp.sin(ar), -jnp.sin(ac), jnp.sin(ac)], axis=1)
    cos_t = jnp.concatenate([cos_c, cos_c], axis=1)
    sin_t = jnp.concatenate([sin_c, sin_c], axis=1)
    cos_t = jnp.concatenate([cos_t, jnp.ones((tm, LANES), F32)], axis=0)
    sin_t = jnp.concatenate([sin_t, jnp.zeros((tm, LANES), F32)], axis=0)
    return cos_t, sin_t


def _rope_lanes(x, cos, sin):
    nf = DIFF_HEAD_DIM // 4
    lane = lax.broadcasted_iota(jnp.int32, x.shape, 1)
    partner = jnp.where((lane % (2 * nf)) < nf,
                        pltpu.roll(x, LANES - nf, axis=1), pltpu.roll(x, nf, axis=1))
    return x * cos + partner * sin


def _diff_qkv_epilogue(D, y, extra, outs):
    cos, sin = extra[0][...], extra[1][...]
    q_ref, k_ref, v_ref = outs
    qscale = DIFF_HEAD_DIM ** -0.5
    for h in range(DIFF_HEADS):
        sl = slice(h * LANES, (h + 1) * LANES)
        q_ref[h] = (_rope_lanes(y[:, sl], cos, sin) * qscale).astype(BF16)
        k_ref[h] = _rope_lanes(y[:, D + h * LANES:D + (h + 1) * LANES], cos, sin).astype(BF16)
        v_ref[h] = y[:, 2 * D + h * LANES:2 * D + (h + 1) * LANES].astype(BF16)


def _diff_attn_kernel(lam_init, n_seg, lam_ref, subg_ref, q_ref, *refs):
    kv = refs[:2 * n_seg]
    o_ref = refs[2 * n_seg]
    lam = lam_ref[...]
    lam_full = (jnp.exp(jnp.sum(lam[0:1] * lam[1:2], axis=-1, keepdims=True))
                - jnp.exp(jnp.sum(lam[2:3] * lam[3:4], axis=-1, keepdims=True)) + lam_init)
    q = q_ref[0]
    lane = lax.broadcasted_iota(jnp.int32, q.shape, 1)
    zero = jnp.zeros_like(q)
    probs = []
    for c in range(2):
        qc = jnp.where((lane >= c * DIFF_HEAD_DIM) & (lane < (c + 1) * DIFF_HEAD_DIM), q, zero)
        s = [_dot_nt(qc, kv[2 * j][0]) for j in range(n_seg)]
        m = functools.reduce(jnp.maximum, [jnp.max(t, axis=-1, keepdims=True) for t in s])
        e = [jnp.exp(t - m) for t in s]
        inv = 1.0 / functools.reduce(jnp.add, [jnp.sum(t, axis=-1, keepdims=True) for t in e])
        probs.append([t * inv for t in e])
    o = None
    for j in range(n_seg):
        a = (probs[0][j] - lam_full * probs[1][j]).astype(BF16)
        t = _dot(a, kv[2 * j + 1][0])
        o = t if o is None else o + t
    o = o * lax.rsqrt(jnp.mean(o * o, axis=-1, keepdims=True) + EPS)
    o_ref[0] = o * subg_ref[...] * (1.0 - lam_init)


def _diff_attention(dims, q, k, v, lam, subg, lam_init, prev, latent, tq):
    H = DIFF_HEADS
    B, L, C = dims.B, dims.L, dims.C
    cblk0 = dims.TL // C
    if latent:
        nq = L // tq
        q_map = lambda b, h, i: (h, b * nq + i, 0)
        kv_specs = [
            pl.BlockSpec((1, L, LANES), lambda b, h, i: (h, b, 0)),
            pl.BlockSpec((1, L, LANES), lambda b, h, i: (h, b, 0)),
            pl.BlockSpec((1, C, LANES), lambda b, h, i: (h, cblk0 + b, 0)),
            pl.BlockSpec((1, C, LANES), lambda b, h, i: (h, cblk0 + b, 0)),
        ]
        kv_args = [k, v, k, v]
    else:
        nq = C // tq
        qblk0 = dims.TL // tq
        q_map = lambda b, h, i: (h, qblk0 + b * nq + i, 0)
        kv_specs = [
            pl.BlockSpec((1, C, LANES), lambda b, h, i: (h, cblk0 + b, 0)),
            pl.BlockSpec((1, C, LANES), lambda b, h, i: (h, cblk0 + b, 0)),
        ]
        kv_args = [k, v]
    n_seg = len(kv_args) // 2
    in_specs = [
        pl.BlockSpec(lam.shape, lambda b, h, i: (0, 0)),
        pl.BlockSpec((1, LANES), lambda b, h, i: (0, 0)),
        pl.BlockSpec((1, tq, LANES), q_map),
        *kv_specs,
    ]
    args = [lam, subg.reshape(1, LANES), q, *kv_args]
    kern = functools.partial(_diff_attn_kernel, lam_init, n_seg)
    aliases = {}
    if prev is not None:
        in_specs.append(pl.BlockSpec(memory_space=pl.ANY))
        args.append(prev)
        aliases = {len(args) - 1: 0}
        base = kern
        kern = lambda *r: base(*r[:-2], r[-1])
    return pl.pallas_call(
        kern,
        grid=(B, H, nq),
        in_specs=in_specs,
        out_specs=pl.BlockSpec((1, tq, LANES), q_map),
        out_shape=jax.ShapeDtypeStruct((H, dims.T, LANES), F32),
        input_output_aliases=aliases,
        compiler_params=_cparams("parallel", "parallel", "parallel"),
        name="diff_attn",
    )(*args)


def _heads_to_lanes(ref):
    return jnp.concatenate([ref[h] for h in range(ref.shape[0])], axis=1)


def _diff_layer(dims, X, g, tabs, w_in, lam, subg, w_out, depth_idx):
    D, H = dims.D, DIFF_HEADS
    tm = 256
    cos_t, sin_t = _rope_tables(dims, tm)
    n_lat, bpb = dims.TL // tm, dims.L // tm
    rope_spec = pl.BlockSpec((tm, LANES), lambda i: (jnp.where(i < n_lat, i % bpb, bpb), 0))
    hm = jax.ShapeDtypeStruct((H, dims.T, LANES), BF16)
    hm_spec = pl.BlockSpec((H, tm, LANES), lambda i: (0, i, 0))
    q, k, v = _norm_mod_proj(
        dims, X, g, tabs[0], tabs[1], w_in.astype(BF16), functools.partial(_diff_qkv_epilogue, D),
        (hm, hm, hm), (hm_spec, hm_spec, hm_spec),
        extra=(cos_t, sin_t), extra_specs=(rope_spec, rope_spec), tm=tm, name="diff_in")
    lam_init = 0.8 - 0.6 * math.exp(-0.3 * depth_idx)
    o = _diff_attention(dims, q, k, v, lam, subg, lam_init, None, True, min(256, dims.L))
    o = _diff_attention(dims, q, k, v, lam, subg, lam_init, o, False, min(256, dims.C))
    return _out_proj(dims, dims.T, [o], [pl.BlockSpec((H, 256, LANES), lambda i: (0, i, 0))],
                     _heads_to_lanes, w_out.astype(BF16), tabs[2], X, name="diff_out")


def _conv_silu_kernel(width, x_ref, w_ref, b_ref, o_ref):
    x = x_ref[...]
    n = x.shape[0]
    row = lax.broadcasted_iota(jnp.int32, x.shape, 0)
    pad = (width - 1) // 2
    acc = jnp.zeros_like(x) + b_ref[...]
    for t in range(width):
        off = t - pad
        xs = x if off == 0 else pltpu.roll(x, (-off) % n, axis=0)
        ok = (row + off >= 0) & (row + off < n)
        acc = acc + jnp.where(ok, xs, 0.0) * w_ref[t:t + 1, :]
    o_ref[...] = _silu(acc)


def _conv_silu(dims, xbc, cw, cb, prev, seq_len, row0):
    n_ch = xbc.shape[1]
    tc = 512
    seq0 = row0 // seq_len
    width = cw.shape[0]
    in_specs = [
        pl.BlockSpec((seq_len, tc), lambda b, j: (seq0 + b, j)),
        pl.BlockSpec((width, tc), lambda b, j: (0, j)),
        pl.BlockSpec((1, tc), lambda b, j: (0, j)),
    ]
    args = [xbc, cw, cb.reshape(1, n_ch)]
    kern = functools.partial(_conv_silu_kernel, width)
    aliases = {}
    if prev is not None:
        in_specs.append(pl.BlockSpec(memory_space=pl.ANY))
        args.append(prev)
        aliases = {3: 0}
        kern = lambda x, w, b, _p, o: _conv_silu_kernel(width, x, w, b, o)
    return pl.pallas_call(
        kern,
        grid=(dims.B, n_ch // tc),
        in_specs=in_specs,
        out_specs=pl.BlockSpec((seq_len, tc), lambda b, j: (seq0 + b, j)),
        out_shape=jax.ShapeDtypeStruct(xbc.shape, F32),
        input_output_aliases=aliases,
        compiler_params=_cparams("parallel", "parallel"),
        name="ssd_conv",
    )(*args)


def _tri(n, reverse):
    r = lax.broadcasted_iota(jnp.int32, (n, n), 0)
    c = lax.broadcasted_iota(jnp.int32, (n, n), 1)
    return (c >= r) if reverse else (c <= r)


def _ssd_scan_kernel(nc, n_heads, a_ref, bias_ref, expand_ref,
                     xs_f, bm_f, cm_f, dt_f, xs_b, bm_b, cm_b, dt_b, s0_ref,
                     of_ref, ob_ref, sout_ref, st_ref):
    s = pl.program_id(1)

    @pl.when(s == 0)
    def _():
        st_ref[...] = s0_ref[0]

    G = SSD_GROUPS
    N = SSD_STATE
    P = SSD_HEAD_DIM
    hpg = n_heads // G
    expand = expand_ref[...]
    for d, (xs_ref, bm_ref, cm_ref, dt_ref, o_ref) in enumerate(
            ((xs_f, bm_f, cm_f, dt_f, of_ref), (xs_b, bm_b, cm_b, dt_b, ob_ref))):
        rev = d == 1
        Q = xs_ref.shape[0]
        mask = _tri(Q, rev)
        tri = jnp.where(mask, 1.0, 0.0).astype(BF16)
        lane = lax.broadcasted_iota(jnp.int32, (Q, LANES), 1)
        mine = (lane >= d * n_heads) & (lane < (d + 1) * n_heads)
        dt = jnp.where(mine, _softplus(dt_ref[...] + bias_ref[...]), 0.0)
        if d == 1:
            dt = pltpu.roll(dt, LANES - n_heads, axis=1)
        la = dt * a_ref[d:d + 1, :]
        cs = _ones_dot(tri, la)
        tot = cs[Q - 1:Q, :] if not rev else cs[0:1, :]
        csT = cs.T
        dtT = dt.T
        ecs = jnp.exp(cs)
        wend = dt * jnp.exp(tot - cs)
        ecs_x = _dot(ecs.astype(BF16), expand) + _dot((ecs - ecs.astype(BF16).astype(F32)).astype(BF16), expand)
        wend_x = _dot(wend.astype(BF16), expand) + _dot((wend - wend.astype(BF16).astype(F32)).astype(BF16), expand)
        etot = jnp.exp(tot)
        etot_x = _dot(etot.astype(BF16), expand) + _dot((etot - etot.astype(BF16).astype(F32)).astype(BF16), expand)
        xs = xs_ref[...]
        xw = (xs * wend_x).astype(BF16)
        xs_bf = xs.astype(BF16)
        outs = []
        for g in range(G):
            cg = cm_ref[:, g * N:(g + 1) * N].astype(BF16)
            bg = bm_ref[:, g * N:(g + 1) * N].astype(BF16)
            cb = _dot_nt(cg, bg)
            gs = slice(g * hpg * P, (g + 1) * hpg * P)
            st = st_ref[d, g]
            inter = _dot(cg, st.astype(BF16)) * ecs_x[:, gs]
            intra = []
            for hh in range(hpg):
                h = g * hpg + hh
                seg = cs[:, h:h + 1] - csT[h:h + 1, :]
                m = cb * jnp.exp(jnp.where(mask, seg, -jnp.inf)) * dtT[h:h + 1, :]
                intra.append(_dot(m.astype(BF16), xs_bf[:, h * P:(h + 1) * P]))
            outs.append(inter + jnp.concatenate(intra, axis=1))
            st_ref[d, g] = st * etot_x[:, gs] + _dot_tn(bg, xw[:, gs])
        o_ref[...] = jnp.concatenate(outs, axis=1)

    @pl.when(s == nc - 1)
    def _():
        sout_ref[0] = st_ref[...]


def _ssd_scan(dims, xbc_act, dt, a, bias, s0, prev, seq_len, row0, q):
    d_inner = xbc_act.shape[1] - 2 * SSD_GROUPS * SSD_STATE
    n_heads = d_inner // SSD_HEAD_DIM
    G, N = SSD_GROUPS, SSD_STATE
    gn = G * N
    nc = seq_len // q
    blk0 = row0 // q
    fwd = lambda b, s: blk0 + b * nc + s
    bwd = lambda b, s: blk0 + b * nc + (nc - 1 - s)
    xs_c, bm_c, cm_c = 0, d_inner // gn, d_inner // gn + 1

    def stream(pos):
        return [
            pl.BlockSpec((q, d_inner), lambda b, s: (pos(b, s), xs_c)),
            pl.BlockSpec((q, gn), lambda b, s: (pos(b, s), bm_c)),
            pl.BlockSpec((q, gn), lambda b, s: (pos(b, s), cm_c)),
            pl.BlockSpec((q, LANES), lambda b, s: (pos(b, s), 0)),
        ]

    expand = (jnp.arange(LANES)[:, None] == (jnp.arange(d_inner)[None, :] // SSD_HEAD_DIM)).astype(BF16)
    a_pad = jnp.zeros((2, LANES), F32).at[:, :n_heads].set(a)
    bias_pad = jnp.zeros((1, LANES), F32).at[0, :2 * n_heads].set(bias.reshape(-1))
    st_shape = (2, G, N, d_inner // G)
    in_specs = [
        pl.BlockSpec((2, LANES), lambda b, s: (0, 0)),
        pl.BlockSpec((1, LANES), lambda b, s: (0, 0)),
        pl.BlockSpec(expand.shape, lambda b, s: (0, 0)),
        *stream(fwd), *stream(bwd),
        pl.BlockSpec((1,) + st_shape, lambda b, s: (b, 0, 0, 0, 0)),
    ]
    args = [a_pad, bias_pad, expand, xbc_act, xbc_act, xbc_act, dt, xbc_act, xbc_act, xbc_act, dt, s0]
    kern = functools.partial(_ssd_scan_kernel, nc, n_heads)
    aliases = {}
    if prev is not None:
        in_specs += [pl.BlockSpec(memory_space=pl.ANY)] * 2
        args += list(prev)
        aliases = {len(args) - 2: 0, len(args) - 1: 1}
        base = kern
        kern = lambda *r: base(*r[:12], *r[14:])
    o_shape = jax.ShapeDtypeStruct((dims.T, d_inner), F32)
    return pl.pallas_call(
        kern,
        grid=(dims.B, nc),
        in_specs=in_specs,
        out_specs=(
            pl.BlockSpec((q, d_inner), lambda b, s: (fwd(b, s), 0)),
            pl.BlockSpec((q, d_inner), lambda b, s: (bwd(b, s), 0)),
            pl.BlockSpec((1,) + st_shape, lambda b, s: (b, 0, 0, 0, 0)),
        ),
        out_shape=(o_shape, o_shape, jax.ShapeDtypeStruct((dims.B,) + st_shape, F32)),
        scratch_shapes=[pltpu.VMEM(st_shape, F32)],
        input_output_aliases=aliases,
        compiler_params=_cparams("parallel", "arbitrary"),
        name="ssd_scan",
    )(*args)


def _ssd_finish(group, of_ref, ob_ref, xs_ref, z_ref, dskip_ref, ng_ref):
    y = of_ref[...] + ob_ref[...] + xs_ref[...] * dskip_ref[...]
    return _group_rmsnorm(y * _silu(z_ref[...]), ng_ref[...], group)


def _ssd_layer(dims, X, g, tabs, w_in, conv_w, conv_b, a_log, dt_bias, d_skip, norm_g, w_out):
    D = dims.D
    d_inner = w_out.shape[0]
    n_heads = d_inner // SSD_HEAD_DIM
    conv_dim = conv_w.shape[1]
    n_dt = 2 * n_heads
    w_pad = jnp.concatenate([w_in, jnp.zeros((D, LANES - n_dt), w_in.dtype)], axis=1).astype(BF16)

    def epi(y, extra, outs):
        outs[0][...] = y[:, :d_inner]
        outs[1][...] = y[:, d_inner:d_inner + conv_dim]
        outs[2][...] = y[:, d_inner + conv_dim:]

    tm = 256
    row = lambda i: (i, 0)
    z, xbc, dt = _norm_mod_proj(
        dims, X, g, tabs[0], tabs[1], w_pad, epi,
        (jax.ShapeDtypeStruct((dims.T, d_inner), F32), jax.ShapeDtypeStruct((dims.T, conv_dim), F32),
         jax.ShapeDtypeStruct((dims.T, LANES), F32)),
        (pl.BlockSpec((tm, d_inner), row), pl.BlockSpec((tm, conv_dim), row), pl.BlockSpec((tm, LANES), row)),
        tm=tm, name="ssd_in")
    act = _conv_silu(dims, xbc, conv_w, conv_b, None, dims.L, 0)
    act = _conv_silu(dims, xbc, conv_w, conv_b, act, dims.C, dims.TL)
    a = -jnp.exp(a_log.astype(F32))
    q = min(128, dims.C)
    st_shape = (dims.B, 2, SSD_GROUPS, SSD_STATE, d_inner // SSD_GROUPS)
    oc_f, oc_b, s_ctx = _ssd_scan(dims, act, dt, a, dt_bias, jnp.zeros(st_shape, F32), None, dims.C, dims.TL, q)
    o_f, o_b, _ = _ssd_scan(dims, act, dt, a, dt_bias, s_ctx, (oc_f, oc_b), dims.L, 0, q)
    dskip_x = jnp.repeat(d_skip.astype(F32), SSD_HEAD_DIM).reshape(1, d_inner)
    wide = pl.BlockSpec((tm, d_inner), row)
    vec = pl.BlockSpec((1, d_inner), lambda i: (0, 0))
    return _out_proj(
        dims, dims.T, [o_f, o_b, act, z, dskip_x, norm_g.reshape(1, d_inner)],
        [wide, wide, wide, wide, vec, vec],
        functools.partial(_ssd_finish, d_inner // SSD_GROUPS), w_out.astype(BF16), tabs[2], X, tm=tm,
        name="ssd_out")


def _hgrn_scan_kernel(nc, layer_idx, sub, lb_ref, q_f, f_f, v_f, q_b, f_b, v_b, s0_ref,
                      of_ref, ob_ref, sout_ref, st_ref):
    s = pl.program_id(1)

    @pl.when(s == 0)
    def _():
        st_ref[...] = s0_ref[0]

    lbr = lb_ref[...]
    e = jnp.exp(lbr - jnp.max(lbr, axis=0, keepdims=True))
    soft = e / jnp.sum(e, axis=0, keepdims=True)
    lb = jnp.zeros_like(soft[0:1])
    for l in range(1, layer_idx + 1):
        lb = lb + soft[l:l + 1]

    K = HGRN_HEAD_DIM
    n_heads = q_f.shape[1] // K
    for d, (q_ref, f_ref, v_ref, o_ref) in enumerate(((q_f, f_f, v_f, of_ref), (q_b, f_b, v_b, ob_ref))):
        rev = d == 1
        Q = q_ref.shape[0]
        nsub = Q // sub
        r = lax.broadcasted_iota(jnp.int32, (Q, Q), 0)
        c = lax.broadcasted_iota(jnp.int32, (Q, Q), 1)
        mask = (c >= r) if rev else (c <= r)
        same = (r // sub) == (c // sub)
        tri = jnp.where(mask, 1.0, 0.0).astype(BF16)
        tri_blk = jnp.where(mask & same, 1.0, 0.0).astype(BF16)
        f = lb + (1.0 - lb) * _sigmoid(f_ref[...])
        kk = 1.0 - f
        lg = jnp.log(f)
        cs = _ones_dot(tri, lg)
        a = _ones_dot(tri_blk, lg)
        rr = cs - a
        tot = cs[Q - 1:Q, :] if not rev else cs[0:1, :]
        qv = q_ref[...]
        qt = (qv * jnp.exp(a)).astype(BF16)
        qe = (qv * jnp.exp(cs)).astype(BF16)
        kend = (kk * jnp.exp(tot - cs)).astype(BF16)
        etot = jnp.exp(tot)
        v_bf = v_ref[...].astype(BF16)
        outs = []
        for h in range(n_heads):
            hs = slice(h * K, (h + 1) * K)
            w_rows = []
            for i in range(nsub):
                r_i = rr[i * sub:i * sub + 1, hs]
                kt = (kk[:, hs] * jnp.exp(jnp.minimum(r_i - cs[:, hs], EXP_CLAMP))).astype(BF16)
                w_rows.append(_dot_nt(qt[i * sub:(i + 1) * sub, hs], kt))
            w = jnp.where(mask, jnp.concatenate(w_rows, axis=0), 0.0).astype(BF16)
            st = st_ref[d, h]
            outs.append(_dot(w, v_bf[:, hs]) + _dot_nt(qe[:, hs], st.astype(BF16)))
            st_ref[d, h] = st * etot[:, hs] + _dot_tn(v_bf[:, hs], kend[:, hs])
        o_ref[...] = jnp.concatenate(outs, axis=1)

    @pl.when(s == nc - 1)
    def _():
        sout_ref[0] = st_ref[...]


def _hgrn_scan(dims, layer_idx, lb_all, qa, ffw, fbw, va, s0, prev, seq_len, row0, q, sub):
    D = dims.D
    n_heads = D // HGRN_HEAD_DIM
    nc = seq_len // q
    blk0 = row0 // q
    fwd = lambda b, s: (blk0 + b * nc + s, 0)
    bwd = lambda b, s: (blk0 + b * nc + (nc - 1 - s), 0)
    st_shape = (2, n_heads, HGRN_HEAD_DIM, HGRN_HEAD_DIM)
    blk = lambda m: pl.BlockSpec((q, D), m)
    in_specs = [
        pl.BlockSpec(lb_all.shape, lambda b, s: (0, 0)),
        blk(fwd), blk(fwd), blk(fwd), blk(bwd), blk(bwd), blk(bwd),
        pl.BlockSpec((1,) + st_shape, lambda b, s: (b, 0, 0, 0, 0)),
    ]
    args = [lb_all, qa, ffw, va, qa, fbw, va, s0]
    kern = functools.partial(_hgrn_scan_kernel, nc, layer_idx, sub)
    aliases = {}
    if prev is not None:
        in_specs += [pl.BlockSpec(memory_space=pl.ANY)] * 2
        args += list(prev)
        aliases = {len(args) - 2: 0, len(args) - 1: 1}
        base = kern
        kern = lambda *r: base(*r[:8], *r[10:])
    o_shape = jax.ShapeDtypeStruct((dims.T, D), F32)
    return pl.pallas_call(
        kern,
        grid=(dims.B, nc),
        in_specs=in_specs,
        out_specs=(blk(fwd), blk(bwd), pl.BlockSpec((1,) + st_shape, lambda b, s: (b, 0, 0, 0, 0))),
        out_shape=(o_shape, o_shape, jax.ShapeDtypeStruct((dims.B,) + st_shape, F32)),
        scratch_shapes=[pltpu.VMEM(st_shape, F32)],
        input_output_aliases=aliases,
        compiler_params=_cparams("parallel", "arbitrary"),
        name="hgrn_scan",
    )(*args)


def _hgrn_finish(of_ref, ob_ref, g_ref, ng_ref):
    o = _group_rmsnorm(of_ref[...] + ob_ref[...], ng_ref[...], HGRN_HEAD_DIM)
    return o * _silu(g_ref[...])


def _hgrn_layer(dims, X, g, tabs, w_in, lb_all, layer_idx, norm_g, w_out, ctx_out):
    D = dims.D

    def epi(y, extra, outs):
        outs[0][...] = _silu(y[:, :D])
        for j in range(1, 5):
            outs[j][...] = y[:, j * D:(j + 1) * D]

    tm = 256
    row = lambda i: (i, 0)
    shp = jax.ShapeDtypeStruct((dims.T, D), F32)
    spec = pl.BlockSpec((tm, D), row)
    qa, ffw, fbw, va, ga = _norm_mod_proj(
        dims, X, g, tabs[0], tabs[1], w_in.astype(BF16), epi, (shp,) * 5, (spec,) * 5, tm=tm, name="hgrn_in")
    q, sub = min(64, dims.C), 16
    n_heads = D // HGRN_HEAD_DIM
    s0 = jnp.zeros((dims.B, 2, n_heads, HGRN_HEAD_DIM, HGRN_HEAD_DIM), F32)
    oc_f, oc_b, s_ctx = _hgrn_scan(dims, layer_idx, lb_all, qa, ffw, fbw, va, s0, None, dims.C, dims.TL, q, sub)
    o_f, o_b, _ = _hgrn_scan(dims, layer_idx, lb_all, qa, ffw, fbw, va, s_ctx, (oc_f, oc_b), dims.L, 0, q, sub)
    n_rows = dims.T if ctx_out else dims.TL
    return _out_proj(dims, n_rows, [o_f, o_b, ga, norm_g.reshape(1, D)],
                     [spec, spec, spec, pl.BlockSpec((1, D), lambda i: (0, 0))],
                     _hgrn_finish, w_out.astype(BF16), tabs[2], X, tm=tm, name="hgrn_out")


SUBLANES = 8
PACKED_ROWS = 16


def _batcher_pairs(n):
    pairs = []
    p = 1
    while p < n:
        k = p
        while k >= 1:
            for j in range(k % p, n - k, 2 * k):
                for i in range(min(k, n - j - k)):
                    if (i + j) // (2 * p) == (i + j + k) // (2 * p):
                        pairs.append((i + j, i + j + k))
            k //= 2
        p *= 2
    return pairs


def _bitonic_merge_desc(c):
    n = len(c)
    k = n // 2
    while k >= 1:
        for i in range(n):
            if not i & k:
                c[i], c[i + k] = jnp.maximum(c[i], c[i + k]), jnp.minimum(c[i], c[i + k])
        k //= 2
    return c


def _top16_sorted(s):
    k = PEER_TOPK
    x = [s[v * SUBLANES:(v + 1) * SUBLANES] for v in range(s.shape[0] // SUBLANES)]
    assert len(x) == k
    for i, j in _batcher_pairs(k):
        x[i], x[j] = jnp.maximum(x[i], x[j]), jnp.minimum(x[i], x[j])
    shift = SUBLANES // 2
    while shift >= 1:
        y = [pltpu.roll(t, shift, axis=0) for t in x]
        x = _bitonic_merge_desc([jnp.maximum(x[i], y[k - 1 - i]) for i in range(k)])
        shift //= 2
    return x


def _rank_rows(s, k):
    n = s.shape[0]
    rows = lax.broadcasted_iota(jnp.int32, s.shape, 0)
    rank = jnp.full(s.shape, float(k), F32)
    vals = []
    for r in range(k):
        m = jnp.max(s, axis=0, keepdims=True)
        vals.append(m)
        hit = rows == jnp.min(jnp.where(s == m, rows, n), axis=0, keepdims=True)
        rank = jnp.where(hit, float(r), rank)
        s = jnp.where(hit, -jnp.inf, s)
    return rank, vals


def _staircase(sv1, sv2):
    k = PEER_TOPK
    tb = sv1[0].shape[1]
    sub = lax.broadcasted_iota(jnp.int32, (SUBLANES, tb), 0)

    def stack(rows8):
        out = jnp.broadcast_to(rows8[SUBLANES - 1], (SUBLANES, tb))
        for i in range(SUBLANES - 2, -1, -1):
            out = jnp.where(sub == i, rows8[i], out)
        return out

    lo2, hi2, hi1 = stack(sv2[:SUBLANES]), stack(sv2[SUBLANES:]), stack(sv1[SUBLANES:])
    groups = [sv1[0] + lo2, sv1[0] + hi2] + [sv1[a] + lo2 for a in range(1, SUBLANES)] + [hi1 + sv2[0]]
    cand = jnp.concatenate(groups, axis=0)
    rank, best = _rank_rows(cand, k)
    picked = jnp.where(rank < k, 1.0, 0.0)
    cnt = lambda lo, hi: jnp.sum(picked[lo:hi], axis=0, keepdims=True)
    n_sel = [cnt(0, 2 * SUBLANES)]
    n_sel += [cnt((a + 1) * SUBLANES, (a + 2) * SUBLANES) for a in range(1, SUBLANES)]
    base = (SUBLANES + 1) * SUBLANES
    n_sel += [picked[base + i:base + i + 1] for i in range(SUBLANES)]
    z = functools.reduce(jnp.add, [jnp.exp(b - best[0]) for b in best])
    return n_sel, 1.0 / z


def _peer_select(st1, st2, exact_ties):
    k = PEER_TOPK
    if exact_ties:
        rank1, sv1 = _rank_rows(st1, k)
        rank2, sv2 = _rank_rows(st2, k)
        tie = None
    else:
        sv1 = [t[0:1] for t in _top16_sorted(st1)]
        sv2 = [t[0:1] for t in _top16_sorted(st2)]
        rank2 = jnp.full(st2.shape, float(k), F32)
        for r in range(k - 1, -1, -1):
            rank2 = jnp.where(st2 >= sv2[r], float(r), rank2)
        dup = [jnp.where(sv[r] == sv[r + 1], 1.0, 0.0) for sv in (sv1, sv2) for r in range(k - 1)]
        over = [jnp.sum(jnp.where(st >= sv[k - 1], 1.0, 0.0), axis=0, keepdims=True) - k
                for st, sv in ((st1, sv1), (st2, sv2))]
        tie = functools.reduce(jnp.maximum, dup + over)
    n_sel, inv_z = _staircase(sv1, sv2)
    nr = jnp.zeros(st1.shape, F32)
    if exact_ties:
        for a in range(k):
            nr = jnp.where(rank1 == float(a), n_sel[a], nr)
    else:
        for a in range(k - 1, -1, -1):
            nr = jnp.where(st1 >= sv1[a], n_sel[a], nr)
    p1 = jnp.exp(st1 - sv1[0]) * inv_z
    p2 = jnp.exp(st2 - sv2[0])
    return nr, p1, rank2, p2, tie


def _bf16_pair_words(x):
    hi = pltpu.bitcast(x.astype(BF16).astype(F32), jnp.uint32)
    return hi | (hi >> 16)


def _bf16_row_tile(ref, hh, base, ii, lanes):
    start = pl.multiple_of(base + (ii // SUBLANES) * SUBLANES, SUBLANES)
    rows8 = ref[hh, pl.ds(start, SUBLANES), lanes]
    words = jnp.broadcast_to(rows8[ii % SUBLANES:ii % SUBLANES + 1, :], rows8.shape)
    return pltpu.bitcast(words, BF16)


def _peer_pre_kernel(n_heads, x_ref, g_ref, sh_ref, sc_ref, wq_ref, keys_ref,
                     h_ref, nr_ref, p1_ref, r2_ref, p2_ref):
    h = _norm_mod(x_ref[...], g_ref[...], sh_ref[0], sc_ref[0]).astype(BF16)
    h_ref[...] = h
    q = _dot(h, wq_ref[...]).astype(BF16)
    dk = keys_ref.shape[2]
    fast = keys_ref.shape[1] == PEER_TOPK * SUBLANES

    def scores(hh, c):
        j = hh * 2 + c
        return _dot_nt(keys_ref[j], q[:, j * dk:(j + 1) * dk])

    def emit(hh, exact_ties):
        st1, st2 = scores(hh, 0), scores(hh, 1)
        ties = []
        for lo in range(0, st1.shape[1], LANES):
            cols = slice(lo, lo + LANES)
            nr, p1, rank2, p2, tie = _peer_select(st1[:, cols], st2[:, cols], exact_ties)
            nr_ref[hh, :, cols] = _bf16_pair_words(nr)
            p1_ref[hh, :, cols] = _bf16_pair_words(p1)
            r2_ref[hh, :, cols] = rank2.astype(BF16)
            p2_ref[hh, :, cols] = p2.astype(BF16)
            ties.append(tie)
        return None if exact_ties else functools.reduce(jnp.maximum, ties)

    if not fast:
        for hh in range(n_heads):
            emit(hh, True)
        return
    ties = [emit(hh, False) for hh in range(n_heads)]
    any_tie = jnp.max(functools.reduce(jnp.maximum, ties)) > 0.0

    @pl.when(any_tie)
    def _():
        for hh in range(n_heads):
            emit(hh, True)


def _peer_pre(dims, X, n_rows, g, sh_tab, sc_tab, wq, keys, tb):
    D = dims.D
    H, _, n_keys, dk = keys.shape
    keys2 = keys.reshape(H * 2, n_keys, dk).astype(BF16)
    tab_spec = pl.BlockSpec((1, 1, D), lambda i: (dims.mod_row(i, tb), 0, 0))
    s_spec = pl.BlockSpec((H, n_keys, tb), lambda i: (0, 0, i))
    s_f32 = jax.ShapeDtypeStruct((H, n_keys, n_rows), jnp.uint32)
    s_bf16 = jax.ShapeDtypeStruct((H, n_keys, n_rows), BF16)
    return pl.pallas_call(
        functools.partial(_peer_pre_kernel, H),
        grid=(n_rows // tb,),
        in_specs=[
            pl.BlockSpec((tb, D), lambda i: (i, 0)),
            pl.BlockSpec((1, D), lambda i: (0, 0)),
            tab_spec, tab_spec,
            pl.BlockSpec(wq.shape, lambda i: (0, 0)),
            pl.BlockSpec(keys2.shape, lambda i: (0, 0, 0)),
        ],
        out_specs=(pl.BlockSpec((tb, D), lambda i: (i, 0)), s_spec, s_spec, s_spec, s_spec),
        out_shape=(jax.ShapeDtypeStruct((n_rows, D), BF16), s_f32, s_f32, s_bf16, s_bf16),
        compiler_params=_cparams("parallel"),
        name="peer_pre",
    )(X, g.reshape(1, D), sh_tab, sc_tab, wq.astype(BF16), keys2)


PEER_SECOND_K = 256


def _peer_main_kernel(n_chunks, final, h_ref, nr_ref, p1_ref, r2_ref, p2_ref, u_ref, vt_ref, x_ref,
                      gate_ref, fg_ref, o_ref, act_ref, p_ref, acc_ref):
    e = pl.program_id(1)
    n_heads, n_keys, tb = r2_ref.shape
    ec = u_ref.shape[0]

    @pl.when(e == 0)
    def _():
        acc_ref[...] = jnp.zeros_like(acc_ref)

    per_piece = PEER_SECOND_K // n_keys
    groups = n_keys // PACKED_ROWS
    i1_base = e * (ec // n_keys)
    lane_chunk = min(tb, 2 * LANES)
    zero = jnp.zeros((PACKED_ROWS, lane_chunk), BF16)
    for ii in range(ec // n_keys):
        for lo in range(0, tb, lane_chunk):
            lanes = slice(lo, lo + lane_chunk)
            g = [None] * groups
            for hh in range(n_heads):
                nrow = _bf16_row_tile(nr_ref, hh, i1_base, ii, lanes)
                prow = _bf16_row_tile(p1_ref, hh, i1_base, ii, lanes)
                for k in range(groups):
                    rows = slice(k * PACKED_ROWS, (k + 1) * PACKED_ROWS)
                    w = jnp.where(r2_ref[hh, rows, lanes] < nrow, prow * p2_ref[hh, rows, lanes], zero)
                    g[k] = w if g[k] is None else g[k] + w
            for k in range(groups):
                rows = slice(ii * n_keys + k * PACKED_ROWS, ii * n_keys + (k + 1) * PACKED_ROWS)
                p_ref[rows, lanes] = g[k]
    act_ref[...] = _dot_nt(u_ref[...], h_ref[...])
    for r in range(0, ec, PACKED_ROWS):
        rows = slice(r, r + PACKED_ROWS)
        p_ref[rows, :] = p_ref[rows, :] * _gelu_exact(act_ref[rows, :]).astype(BF16)
    acc_ref[...] += _dot(vt_ref[...], p_ref[...])

    @pl.when(e == n_chunks - 1)
    def _():
        out = x_ref[...] + gate_ref[0] * acc_ref[...].T
        if final:
            out = out * lax.rsqrt(jnp.mean(out * out, axis=-1, keepdims=True) + EPS) * fg_ref[...]
        o_ref[...] = out


def _peer_layer(dims, X, n_rows, g, tabs, wq, keys, u, v, final_g):
    D = dims.D
    H, _, n_keys, _ = keys.shape
    n_exp = u.shape[0]
    h2, nr, p1, r2, p2 = _peer_pre(dims, X, n_rows, g, tabs[3], tabs[4], wq, keys, dims.tile(n_rows, 256))
    tb = dims.tile(n_rows, 512)
    ec = min(1024, n_exp)
    assert ec % PEER_SECOND_K == 0 and PEER_SECOND_K % n_keys == 0 and n_keys % PACKED_ROWS == 0
    assert (ec // n_keys) % SUBLANES == 0
    n_chunks = n_exp // ec
    u_bf = u.astype(BF16)
    vt_bf = v.astype(BF16).T
    final = final_g is not None
    fg = (final_g if final else jnp.ones((D,), F32)).reshape(1, D)
    s_spec = pl.BlockSpec((H, n_keys, tb), lambda i, e: (0, 0, i))
    return pl.pallas_call(
        functools.partial(_peer_main_kernel, n_chunks, final),
        grid=(n_rows // tb, n_chunks),
        in_specs=[
            pl.BlockSpec((tb, D), lambda i, e: (i, 0)),
            s_spec, s_spec, s_spec, s_spec,
            pl.BlockSpec((ec, D), lambda i, e: (e, 0)),
            pl.BlockSpec((D, ec), lambda i, e: (0, e)),
            pl.BlockSpec((tb, D), lambda i, e: (i, 0)),
            pl.BlockSpec((1, 1, D), lambda i, e: (dims.mod_row(i, tb), 0, 0)),
            pl.BlockSpec((1, D), lambda i, e: (0, 0)),
        ],
        out_specs=pl.BlockSpec((tb, D), lambda i, e: (i, 0)),
        out_shape=jax.ShapeDtypeStruct((n_rows, D), F32),
        scratch_shapes=[pltpu.VMEM((ec, tb), F32), pltpu.VMEM((ec, tb), BF16), pltpu.VMEM((D, tb), F32)],
        compiler_params=_cparams("parallel", "arbitrary"),
        name="peer_main",
    )(h2, nr, p1, r2, p2, u_bf, vt_bf, X, tabs[5], fg)


def kernel(x, c, ctx, c_ctx, w_mod, b_mod, norm_g, four_w_out, diff_w_in, diff_lam, diff_norm_g, diff_w_out,
           ssd_w_in, ssd_conv_w, ssd_conv_b, ssd_a_log, ssd_dt_bias, ssd_d, ssd_norm_g, ssd_w_out, hgrn_w_in,
           hgrn_lb, hgrn_norm_g, hgrn_w_out, peer_wq, peer_keys, peer_u, peer_v, final_g):
    B, L, D = x.shape
    C = ctx.shape[1]
    depth = w_mod.shape[0]
    n_mixers = 4
    dims = Dims(B, L, C, D)
    assert B + 1 <= MOD_ROWS

    X = jnp.concatenate([x.reshape(B * L, D), ctx.reshape(B * C, D)], axis=0)
    cc = jnp.concatenate([c, c_ctx[None, :], jnp.zeros((MOD_ROWS - B - 1, D), c.dtype)], axis=0)
    mods = _mod_tables(cc, w_mod, b_mod)

    for i in range(depth):
        last = i == depth - 1
        kind, j = i % n_mixers, i // n_mixers
        tabs = [mods[i, :, k * D:(k + 1) * D].reshape(MOD_ROWS, 1, D) for k in range(6)]
        g1, g2 = norm_g[i, 0], norm_g[i, 1]
        if kind == 0:
            X = _fourier_layer(dims, X, g1, tabs, four_w_out[j])
        elif kind == 1:
            X = _diff_layer(dims, X, g1, tabs, diff_w_in[j], diff_lam[j], diff_norm_g[j], diff_w_out[j], i)
        elif kind == 2:
            X = _ssd_layer(dims, X, g1, tabs, ssd_w_in[j], ssd_conv_w[j], ssd_conv_b[j], ssd_a_log[j],
                           ssd_dt_bias[j], ssd_d[j], ssd_norm_g[j], ssd_w_out[j])
        else:
            X = _hgrn_layer(dims, X, g1, tabs, hgrn_w_in[j], hgrn_lb, i, hgrn_norm_g[j], hgrn_w_out[j],
                            not last)
        n_rows = dims.TL if last else dims.T
        X = _peer_layer(dims, X, n_rows, g2, tabs, peer_wq[i], peer_keys[i], peer_u[i], peer_v[i],
                        final_g if last else None)
    return X[:B * L].reshape(B, L, D)
```

```python
import functools
import math

import jax
import jax.numpy as jnp
import numpy as np
from jax import lax
from jax.experimental import pallas as pl
from jax.experimental.pallas import tpu as pltpu

F32 = jnp.float32
BF16 = jnp.bfloat16

EPS = 1e-6
ROPE_BASE = 10000.0
GRID_W = 64
FOURIER_GROUPS = 8
DIFF_HEADS = 8
DIFF_HEAD_DIM = 64
SSD_HEAD_DIM = 64
SSD_STATE = 128
SSD_GROUPS = 8
HGRN_HEAD_DIM = 128
PEER_TOPK = 16

LANES = 128
VMEM_LIMIT_BYTES = 56 * 2**20
MOD_ROWS = 16
EXP_CLAMP = 80.0


def _cparams(*sem):
    return pltpu.CompilerParams(dimension_semantics=sem, vmem_limit_bytes=VMEM_LIMIT_BYTES)


class Dims:
    def __init__(self, B, L, C, D):
        self.B, self.L, self.C, self.D = B, L, C, D
        self.TL = B * L
        self.T = B * L + B * C

    def tile(self, n_rows, pref):
        t = pref
        while self.L % t or n_rows % t:
            t //= 2
        return t

    def mod_row(self, i, tm):
        n_lat = self.TL // tm
        return jnp.where(i < n_lat, i // (self.L // tm), self.B)


def _silu(x):
    return x * (1.0 / (1.0 + jnp.exp(-x)))


def _sigmoid(x):
    return 1.0 / (1.0 + jnp.exp(-x))


def _softplus(x):
    return jnp.maximum(x, 0.0) + jnp.log(1.0 + jnp.exp(-jnp.abs(x)))


def _gelu_exact(x):
    return 0.5 * x * (1.0 + lax.erf(x * (1.0 / math.sqrt(2.0))))


def _norm_mod(x, g, shift, scale):
    y = x * lax.rsqrt(jnp.mean(x * x, axis=-1, keepdims=True) + EPS)
    return (y * g) * (1.0 + scale) + shift


def _group_rmsnorm(y, g, group):
    parts = []
    for s in range(0, y.shape[1], group):
        p = y[:, s:s + group]
        parts.append(p * lax.rsqrt(jnp.mean(p * p, axis=-1, keepdims=True) + EPS))
    return jnp.concatenate(parts, axis=1) * g


def _split3(x):
    a = x.astype(BF16)
    r = x - a.astype(F32)
    b = r.astype(BF16)
    c = (r - b.astype(F32)).astype(BF16)
    return a, b, c


def _dot(a, b):
    return jnp.dot(a, b, preferred_element_type=F32)


def _dot_nt(a, b):
    return lax.dot_general(a, b, (((1,), (1,)), ((), ())), preferred_element_type=F32)


def _dot_tn(a, b):
    return lax.dot_general(a, b, (((0,), (0,)), ((), ())), preferred_element_type=F32)


def _ones_dot(mat01, x):
    a, b, c = _split3(x)
    return _dot(mat01, a) + _dot(mat01, b) + _dot(mat01, c)


def _mod_kernel(c_ref, w_ref, b_ref, o_ref):
    h = _silu(c_ref[...]).astype(BF16)
    o_ref[0] = _dot(h, w_ref[0].astype(BF16)) + b_ref[0]


def _mod_tables(cc, w_mod, b_mod):
    depth, D, N = w_mod.shape
    tn = 1536
    return pl.pallas_call(
        _mod_kernel,
        grid=(depth, N // tn),
        in_specs=[
            pl.BlockSpec((MOD_ROWS, D), lambda l, j: (0, 0)),
            pl.BlockSpec((1, D, tn), lambda l, j: (l, 0, j)),
            pl.BlockSpec((1, 1, tn), lambda l, j: (l, 0, j)),
        ],
        out_specs=pl.BlockSpec((1, MOD_ROWS, tn), lambda l, j: (l, 0, j)),
        out_shape=jax.ShapeDtypeStruct((depth, MOD_ROWS, N), F32),
        compiler_params=_cparams("parallel", "parallel"),
        name="mod_tables",
    )(cc, w_mod, b_mod.reshape(depth, 1, N))


def _proj_kernel(epilogue, n_extra, x_ref, g_ref, sh_ref, sc_ref, w_ref, *rest):
    h = _norm_mod(x_ref[...], g_ref[...], sh_ref[0], sc_ref[0])
    y = _dot(h.astype(BF16), w_ref[...])
    epilogue(y, rest[:n_extra], rest[n_extra:])


def _norm_mod_proj(dims, x, g, shift_tab, scale_tab, w, epilogue, out_shapes, out_specs,
                   extra=(), extra_specs=(), tm=256, name="proj"):
    D = dims.D
    n_rows = x.shape[0]
    tab_spec = pl.BlockSpec((1, 1, D), lambda i: (dims.mod_row(i, tm), 0, 0))
    return pl.pallas_call(
        functools.partial(_proj_kernel, epilogue, len(extra)),
        grid=(n_rows // tm,),
        in_specs=[
            pl.BlockSpec((tm, D), lambda i: (i, 0)),
            pl.BlockSpec((1, D), lambda i: (0, 0)),
            tab_spec, tab_spec,
            pl.BlockSpec(w.shape, lambda i: (0, 0)),
            *extra_specs,
        ],
        out_specs=out_specs,
        out_shape=out_shapes,
        compiler_params=_cparams("parallel"),
        name=name,
    )(x, g.reshape(1, D), shift_tab, scale_tab, w, *extra)


def _out_proj_kernel(prologue, n_in, *refs):
    ins = refs[:n_in]
    w_ref, gate_ref, x_ref, o_ref = refs[n_in:]
    y = prologue(*ins)
    o_ref[...] = x_ref[...] + gate_ref[0] * _dot(y.astype(BF16), w_ref[...])


def _out_proj(dims, n_rows, ins, in_specs, prologue, w, gate_tab, x, tm=256, name="out_proj"):
    D = dims.D
    return pl.pallas_call(
        functools.partial(_out_proj_kernel, prologue, len(ins)),
        grid=(n_rows // tm,),
        in_specs=[
            *in_specs,
            pl.BlockSpec(w.shape, lambda i: (0, 0)),
            pl.BlockSpec((1, 1, D), lambda i: (dims.mod_row(i, tm), 0, 0)),
            pl.BlockSpec((tm, D), lambda i: (i, 0)),
        ],
        out_specs=pl.BlockSpec((tm, D), lambda i: (i, 0)),
        out_shape=jax.ShapeDtypeStruct((n_rows, D), F32),
        compiler_params=_cparams("parallel"),
        name=name,
    )(*ins, w, gate_tab, x)


def _dft_tables(n):
    k = jnp.arange(n, dtype=jnp.int32)
    ang = (2.0 * math.pi / n) * ((k[:, None] * k[None, :]) % n).astype(F32)
    return jnp.cos(ang), jnp.sin(ang)


def _seq_dft_kernel(n_lat, scale_l, scale_c, cl_ref, sl_ref, hl_ref, cc_ref, sc_ref, hc_ref, o_ref):
    i = pl.program_id(1)
    n = hl_ref.shape[1] // 2

    @pl.when(i < n_lat)
    def _():
        acc = _dot(cl_ref[...], hl_ref[:, :n]) - _dot(sl_ref[...], hl_ref[:, n:])
        o_ref[...] = acc * scale_l

    @pl.when(i == n_lat)
    def _():
        acc = _dot(cc_ref[...], hc_ref[:, :n]) - _dot(sc_ref[...], hc_ref[:, n:])
        o_ref[...] = acc * scale_c


def _seq_dft(dims, hcs, group_width):
    D, L, C = dims.D, dims.L, dims.C
    assert L % C == 0
    n_lat = L // C
    cblk0 = dims.TL // C
    cl, sl = _dft_tables(L)
    cc, sc = _dft_tables(C)
    lat_blk = lambda b, i: (jnp.minimum(i, n_lat - 1), 0)
    return pl.pallas_call(
        functools.partial(_seq_dft_kernel, n_lat, 1.0 / math.sqrt(L * group_width),
                          1.0 / math.sqrt(C * group_width)),
        grid=(dims.B, n_lat + 1),
        in_specs=[
            pl.BlockSpec((C, L), lat_blk),
            pl.BlockSpec((C, L), lat_blk),
            pl.BlockSpec((L, 2 * D), lambda b, i: (b, 0)),
            pl.BlockSpec((C, C), lambda b, i: (0, 0)),
            pl.BlockSpec((C, C), lambda b, i: (0, 0)),
            pl.BlockSpec((C, 2 * D), lambda b, i: (cblk0 + b, 0)),
        ],
        out_specs=pl.BlockSpec((C, D), lambda b, i: (jnp.where(i < n_lat, b * n_lat + i, cblk0 + b), 0)),
        out_shape=jax.ShapeDtypeStruct((dims.T, D), F32),
        compiler_params=_cparams("parallel", "arbitrary"),
        name="seq_dft",
    )(cl.astype(BF16), sl.astype(BF16), hcs, cc.astype(BF16), sc.astype(BF16), hcs)


def _fourier_layer(dims, X, g, tabs, w_out):
    D = dims.D
    gw = D // FOURIER_GROUPS
    cg, sg = _dft_tables(gw)
    eye = jnp.eye(FOURIER_GROUPS, dtype=F32)
    wcs = jnp.concatenate([jnp.kron(eye, cg), jnp.kron(eye, sg)], axis=1).astype(BF16)

    def epi(y, extra, outs):
        outs[0][...] = y.astype(BF16)

    hcs = _norm_mod_proj(
        dims, X, g, tabs[0], tabs[1], wcs, epi,
        jax.ShapeDtypeStruct((dims.T, 2 * D), BF16),
        pl.BlockSpec((256, 2 * D), lambda i: (i, 0)), name="fourier_in")
    f = _seq_dft(dims, hcs, gw)
    return _out_proj(dims, dims.T, [f], [pl.BlockSpec((256, D), lambda i: (i, 0))],
                     lambda r: r[...], w_out.astype(BF16), tabs[2], X, name="fourier_out")


def _rope_tables(dims, tm):
    L, Dh = dims.L, DIFF_HEAD_DIM
    nf = Dh // 4
    pos = jnp.arange(L)
    row = (pos // GRID_W).astype(F32)
    col = (pos % GRID_W).astype(F32)
    freqs = ROPE_BASE ** (-jnp.arange(nf, dtype=F32) / nf)
    ar, ac = row[:, None] * freqs, col[:, None] * freqs
    cos_c = jnp.concatenate([jnp.cos(ar), jnp.cos(ar), jnp.cos(ac), jnp.cos(ac)], axis=1)
    sin_c = jnp.concatenate([-jnp.sin(ar), jnp.sin(ar), -jnp.sin(ac), jnp.sin(ac)], axis=1)
    cos_t = jnp.concatenate([cos_c, cos_c], axis=1)
    sin_t = jnp.concatenate([sin_c, sin_c], axis=1)
    cos_t = jnp.concatenate([cos_t, jnp.ones((tm, LANES), F32)], axis=0)
    sin_t = jnp.concatenate([sin_t, jnp.zeros((tm, LANES), F32)], axis=0)
    return cos_t, sin_t


def _rope_lanes(x, cos, sin):
    nf = DIFF_HEAD_DIM // 4
    lane = lax.broadcasted_iota(jnp.int32, x.shape, 1)
    partner = jnp.where((lane % (2 * nf)) < nf,
                        pltpu.roll(x, LANES - nf, axis=1), pltpu.roll(x, nf, axis=1))
    return x * cos + partner * sin


def _diff_qkv_epilogue(D, y, extra, outs):
    cos, sin = extra[0][...], extra[1][...]
    q_ref, k_ref, v_ref = outs
    qscale = DIFF_HEAD_DIM ** -0.5
    for h in range(DIFF_HEADS):
        sl = slice(h * LANES, (h + 1) * LANES)
        q_ref[h] = (_rope_lanes(y[:, sl], cos, sin) * qscale).astype(BF16)
        k_ref[h] = _rope_lanes(y[:, D + h * LANES:D + (h + 1) * LANES], cos, sin).astype(BF16)
        v_ref[h] = y[:, 2 * D + h * LANES:2 * D + (h + 1) * LANES].astype(BF16)


def _diff_attn_rows(lam_init, lam_ref, subg_ref, q_ref, kv, o_ref):
    lam = lam_ref[...]
    lam_full = (jnp.exp(jnp.sum(lam[0:1] * lam[1:2], axis=-1, keepdims=True))
                - jnp.exp(jnp.sum(lam[2:3] * lam[3:4], axis=-1, keepdims=True)) + lam_init)
    q = q_ref[0]
    lane = lax.broadcasted_iota(jnp.int32, q.shape, 1)
    zero = jnp.zeros_like(q)
    probs = []
    for c in range(2):
        qc = jnp.where((lane >= c * DIFF_HEAD_DIM) & (lane < (c + 1) * DIFF_HEAD_DIM), q, zero)
        s = [_dot_nt(qc, k_ref[0]) for k_ref, _ in kv]
        m = functools.reduce(jnp.maximum, [jnp.max(t, axis=-1, keepdims=True) for t in s])
        e = [jnp.exp(t - m) for t in s]
        inv = 1.0 / functools.reduce(jnp.add, [jnp.sum(t, axis=-1, keepdims=True) for t in e])
        probs.append([t * inv for t in e])
    o = None
    for j, (_, v_ref) in enumerate(kv):
        a = (probs[0][j] - lam_full * probs[1][j]).astype(BF16)
        t = _dot(a, v_ref[0])
        o = t if o is None else o + t
    o = o * lax.rsqrt(jnp.mean(o * o, axis=-1, keepdims=True) + EPS)
    o_ref[0] = o * subg_ref[...] * (1.0 - lam_init)


def _diff_attn_kernel(n_lat, lam_init, lam_ref, subg_ref, q_ref, kl_ref, vl_ref, kc_ref, vc_ref, o_ref):
    i = pl.program_id(2)

    @pl.when(i < n_lat)
    def _():
        _diff_attn_rows(lam_init, lam_ref, subg_ref, q_ref, [(kl_ref, vl_ref), (kc_ref, vc_ref)], o_ref)

    @pl.when(i == n_lat)
    def _():
        _diff_attn_rows(lam_init, lam_ref, subg_ref, q_ref, [(kc_ref, vc_ref)], o_ref)


def _diff_attention(dims, q, k, v, lam, subg, lam_init):
    H = DIFF_HEADS
    B, L, C = dims.B, dims.L, dims.C
    assert L % C == 0
    n_lat = L // C
    cblk0 = dims.TL // C
    q_map = lambda b, h, i: (h, jnp.where(i < n_lat, b * n_lat + i, cblk0 + b), 0)
    lat = pl.BlockSpec((1, L, LANES), lambda b, h, i: (h, b, 0))
    ctx = pl.BlockSpec((1, C, LANES), lambda b, h, i: (h, cblk0 + b, 0))
    return pl.pallas_call(
        functools.partial(_diff_attn_kernel, n_lat, lam_init),
        grid=(B, H, n_lat + 1),
        in_specs=[
            pl.BlockSpec(lam.shape, lambda b, h, i: (0, 0)),
            pl.BlockSpec((1, LANES), lambda b, h, i: (0, 0)),
            pl.BlockSpec((1, C, LANES), q_map),
            lat, lat, ctx, ctx,
        ],
        out_specs=pl.BlockSpec((1, C, LANES), q_map),
        out_shape=jax.ShapeDtypeStruct((H, dims.T, LANES), F32),
        compiler_params=_cparams("parallel", "parallel", "arbitrary"),
        name="diff_attn",
    )(lam, subg.reshape(1, LANES), q, k, v, k, v)


def _heads_to_lanes(ref):
    return jnp.concatenate([ref[h] for h in range(ref.shape[0])], axis=1)


def _diff_layer(dims, X, g, tabs, w_in, lam, subg, w_out, depth_idx):
    D, H = dims.D, DIFF_HEADS
    tm = 256
    cos_t, sin_t = _rope_tables(dims, tm)
    n_lat, bpb = dims.TL // tm, dims.L // tm
    rope_spec = pl.BlockSpec((tm, LANES), lambda i: (jnp.where(i < n_lat, i % bpb, bpb), 0))
    hm = jax.ShapeDtypeStruct((H, dims.T, LANES), BF16)
    hm_spec = pl.BlockSpec((H, tm, LANES), lambda i: (0, i, 0))
    q, k, v = _norm_mod_proj(
        dims, X, g, tabs[0], tabs[1], w_in.astype(BF16), functools.partial(_diff_qkv_epilogue, D),
        (hm, hm, hm), (hm_spec, hm_spec, hm_spec),
        extra=(cos_t, sin_t), extra_specs=(rope_spec, rope_spec), tm=tm, name="diff_in")
    lam_init = 0.8 - 0.6 * math.exp(-0.3 * depth_idx)
    o = _diff_attention(dims, q, k, v, lam, subg, lam_init)
    return _out_proj(dims, dims.T, [o], [pl.BlockSpec((H, 256, LANES), lambda i: (0, i, 0))],
                     _heads_to_lanes, w_out.astype(BF16), tabs[2], X, name="diff_out")


def _conv_silu_kernel(width, n_lat, x_ref, prev_ref, next_ref, w_ref, b_ref, o_ref):
    r = pl.program_id(2)
    x = x_ref[...]
    rb = x.shape[0]
    pad = (width - 1) // 2
    assert pad <= SUBLANES
    before = jnp.where((r > 0) & (r < n_lat), prev_ref[...], 0.0)
    after = jnp.where(r < n_lat - 1, next_ref[...], 0.0)
    xp = jnp.concatenate([before, x, after], axis=0)
    n = xp.shape[0]
    acc = jnp.zeros_like(x) + b_ref[...]
    for t in range(width):
        off = t - pad
        xs = xp if off == 0 else pltpu.roll(xp, (-off) % n, axis=0)
        acc = acc + xs[SUBLANES:SUBLANES + rb] * w_ref[t:t + 1, :]
    o_ref[...] = _silu(acc)


def _conv_silu(dims, xbc, cw, cb):
    n_ch = xbc.shape[1]
    L, C = dims.L, dims.C
    assert L % C == 0 and C % SUBLANES == 0
    tc = min(2048, n_ch)
    n_lat = L // C
    cblk0 = dims.TL // C
    per8 = C // SUBLANES
    last8 = dims.T // SUBLANES - 1
    width = cw.shape[0]
    blk = lambda b, r: jnp.where(r < n_lat, b * n_lat + r, cblk0 + b)
    return pl.pallas_call(
        functools.partial(_conv_silu_kernel, width, n_lat),
        grid=(dims.B, n_ch // tc, n_lat + 1),
        in_specs=[
            pl.BlockSpec((C, tc), lambda b, j, r: (blk(b, r), j)),
            pl.BlockSpec((SUBLANES, tc), lambda b, j, r: (jnp.maximum(blk(b, r) * per8 - 1, 0), j)),
            pl.BlockSpec((SUBLANES, tc), lambda b, j, r: (jnp.minimum((blk(b, r) + 1) * per8, last8), j)),
            pl.BlockSpec((width, tc), lambda b, j, r: (0, j)),
            pl.BlockSpec((1, tc), lambda b, j, r: (0, j)),
        ],
        out_specs=pl.BlockSpec((C, tc), lambda b, j, r: (blk(b, r), j)),
        out_shape=jax.ShapeDtypeStruct(xbc.shape, F32),
        compiler_params=_cparams("parallel", "parallel", "arbitrary"),
        name="ssd_conv",
    )(xbc, xbc, xbc, cw, cb.reshape(1, n_ch))


def _tri(n, reverse):
    r = lax.broadcasted_iota(jnp.int32, (n, n), 0)
    c = lax.broadcasted_iota(jnp.int32, (n, n), 1)
    return (c >= r) if reverse else (c <= r)


def _ssd_scan_kernel(n_heads, a_ref, bias_ref, expand_ref,
                     xs_f, bm_f, cm_f, dt_f, xs_b, bm_b, cm_b, dt_b,
                     of_ref, ob_ref, st_ref):
    s = pl.program_id(1)

    @pl.when(s == 0)
    def _():
        st_ref[...] = jnp.zeros_like(st_ref)

    G = SSD_GROUPS
    N = SSD_STATE
    P = SSD_HEAD_DIM
    hpg = n_heads // G
    expand = expand_ref[...]
    for d, (xs_ref, bm_ref, cm_ref, dt_ref, o_ref) in enumerate(
            ((xs_f, bm_f, cm_f, dt_f, of_ref), (xs_b, bm_b, cm_b, dt_b, ob_ref))):
        rev = d == 1
        Q = xs_ref.shape[0]
        mask = _tri(Q, rev)
        tri = jnp.where(mask, 1.0, 0.0).astype(BF16)
        lane = lax.broadcasted_iota(jnp.int32, (Q, LANES), 1)
        mine = (lane >= d * n_heads) & (lane < (d + 1) * n_heads)
        dt = jnp.where(mine, _softplus(dt_ref[...] + bias_ref[...]), 0.0)
        if d == 1:
            dt = pltpu.roll(dt, LANES - n_heads, axis=1)
        la = dt * a_ref[d:d + 1, :]
        cs = _ones_dot(tri, la)
        tot = cs[Q - 1:Q, :] if not rev else cs[0:1, :]
        csT = cs.T
        dtT = dt.T
        ecs = jnp.exp(cs)
        wend = dt * jnp.exp(tot - cs)
        ecs_x = _dot(ecs.astype(BF16), expand) + _dot((ecs - ecs.astype(BF16).astype(F32)).astype(BF16), expand)
        wend_x = _dot(wend.astype(BF16), expand) + _dot((wend - wend.astype(BF16).astype(F32)).astype(BF16), expand)
        etot = jnp.exp(tot)
        etot_x = _dot(etot.astype(BF16), expand) + _dot((etot - etot.astype(BF16).astype(F32)).astype(BF16), expand)
        xs = xs_ref[...]
        xw = (xs * wend_x).astype(BF16)
        xs_bf = xs.astype(BF16)
        outs = []
        for g in range(G):
            cg = cm_ref[:, g * N:(g + 1) * N].astype(BF16)
            bg = bm_ref[:, g * N:(g + 1) * N].astype(BF16)
            cb = _dot_nt(cg, bg)
            gs = slice(g * hpg * P, (g + 1) * hpg * P)
            st = st_ref[d, g]
            inter = _dot(cg, st.astype(BF16)) * ecs_x[:, gs]
            intra = []
            for hh in range(hpg):
                h = g * hpg + hh
                seg = cs[:, h:h + 1] - csT[h:h + 1, :]
                m = cb * jnp.exp(jnp.where(mask, seg, -jnp.inf)) * dtT[h:h + 1, :]
                intra.append(_dot(m.astype(BF16), xs_bf[:, h * P:(h + 1) * P]))
            outs.append(inter + jnp.concatenate(intra, axis=1))
            st_ref[d, g] = st * etot_x[:, gs] + _dot_tn(bg, xw[:, gs])
        o_ref[...] = jnp.concatenate(outs, axis=1)


def _scan_positions(dims, q):
    ncc, ncl = dims.C // q, dims.L // q
    cblk0 = dims.TL // q
    fwd = lambda b, s: jnp.where(s < ncc, cblk0 + b * ncc + s, b * ncl + (s - ncc))
    bwd = lambda b, s: jnp.where(s < ncc, cblk0 + b * ncc + (ncc - 1 - s), b * ncl + (ncl - 1 - (s - ncc)))
    return ncc + ncl, fwd, bwd


def _ssd_scan(dims, xbc_act, dt, a, bias, q):
    d_inner = xbc_act.shape[1] - 2 * SSD_GROUPS * SSD_STATE
    n_heads = d_inner // SSD_HEAD_DIM
    G, N = SSD_GROUPS, SSD_STATE
    gn = G * N
    n_steps, fwd, bwd = _scan_positions(dims, q)
    xs_c, bm_c, cm_c = 0, d_inner // gn, d_inner // gn + 1

    def stream(pos):
        return [
            pl.BlockSpec((q, d_inner), lambda b, s: (pos(b, s), xs_c)),
            pl.BlockSpec((q, gn), lambda b, s: (pos(b, s), bm_c)),
            pl.BlockSpec((q, gn), lambda b, s: (pos(b, s), cm_c)),
            pl.BlockSpec((q, LANES), lambda b, s: (pos(b, s), 0)),
        ]

    expand = (jnp.arange(LANES)[:, None] == (jnp.arange(d_inner)[None, :] // SSD_HEAD_DIM)).astype(BF16)
    a_pad = jnp.zeros((2, LANES), F32).at[:, :n_heads].set(a)
    bias_pad = jnp.zeros((1, LANES), F32).at[0, :2 * n_heads].set(bias.reshape(-1))
    st_shape = (2, G, N, d_inner // G)
    in_specs = [
        pl.BlockSpec((2, LANES), lambda b, s: (0, 0)),
        pl.BlockSpec((1, LANES), lambda b, s: (0, 0)),
        pl.BlockSpec(expand.shape, lambda b, s: (0, 0)),
        *stream(fwd), *stream(bwd),
    ]
    args = [a_pad, bias_pad, expand, xbc_act, xbc_act, xbc_act, dt, xbc_act, xbc_act, xbc_act, dt]
    o_shape = jax.ShapeDtypeStruct((dims.T, d_inner), F32)
    return pl.pallas_call(
        functools.partial(_ssd_scan_kernel, n_heads),
        grid=(dims.B, n_steps),
        in_specs=in_specs,
        out_specs=(
            pl.BlockSpec((q, d_inner), lambda b, s: (fwd(b, s), 0)),
            pl.BlockSpec((q, d_inner), lambda b, s: (bwd(b, s), 0)),
        ),
        out_shape=(o_shape, o_shape),
        scratch_shapes=[pltpu.VMEM(st_shape, F32)],
        compiler_params=_cparams("parallel", "arbitrary"),
        name="ssd_scan",
    )(*args)


def _ssd_finish(group, of_ref, ob_ref, xs_ref, z_ref, dskip_ref, ng_ref):
    y = of_ref[...] + ob_ref[...] + xs_ref[...] * dskip_ref[...]
    return _group_rmsnorm(y * _silu(z_ref[...]), ng_ref[...], group)


def _ssd_layer(dims, X, g, tabs, w_in, conv_w, conv_b, a_log, dt_bias, d_skip, norm_g, w_out):
    D = dims.D
    d_inner = w_out.shape[0]
    n_heads = d_inner // SSD_HEAD_DIM
    conv_dim = conv_w.shape[1]
    n_dt = 2 * n_heads
    w_pad = jnp.concatenate([w_in, jnp.zeros((D, LANES - n_dt), w_in.dtype)], axis=1).astype(BF16)

    def epi(y, extra, outs):
        outs[0][...] = y[:, :d_inner]
        outs[1][...] = y[:, d_inner:d_inner + conv_dim]
        outs[2][...] = y[:, d_inner + conv_dim:]

    tm = 256
    row = lambda i: (i, 0)
    z, xbc, dt = _norm_mod_proj(
        dims, X, g, tabs[0], tabs[1], w_pad, epi,
        (jax.ShapeDtypeStruct((dims.T, d_inner), F32), jax.ShapeDtypeStruct((dims.T, conv_dim), F32),
         jax.ShapeDtypeStruct((dims.T, LANES), F32)),
        (pl.BlockSpec((tm, d_inner), row), pl.BlockSpec((tm, conv_dim), row), pl.BlockSpec((tm, LANES), row)),
        tm=tm, name="ssd_in")
    act = _conv_silu(dims, xbc, conv_w, conv_b)
    a = -jnp.exp(a_log.astype(F32))
    o_f, o_b = _ssd_scan(dims, act, dt, a, dt_bias, min(128, dims.C))
    dskip_x = jnp.repeat(d_skip.astype(F32), SSD_HEAD_DIM).reshape(1, d_inner)
    wide = pl.BlockSpec((tm, d_inner), row)
    vec = pl.BlockSpec((1, d_inner), lambda i: (0, 0))
    return _out_proj(
        dims, dims.T, [o_f, o_b, act, z, dskip_x, norm_g.reshape(1, d_inner)],
        [wide, wide, wide, wide, vec, vec],
        functools.partial(_ssd_finish, d_inner // SSD_GROUPS), w_out.astype(BF16), tabs[2], X, tm=tm,
        name="ssd_out")


def _hgrn_scan_kernel(layer_idx, sub, lb_ref, q_f, f_f, v_f, q_b, f_b, v_b, of_ref, ob_ref, st_ref):
    s = pl.program_id(1)

    @pl.when(s == 0)
    def _():
        st_ref[...] = jnp.zeros_like(st_ref)

    lbr = lb_ref[...]
    e = jnp.exp(lbr - jnp.max(lbr, axis=0, keepdims=True))
    soft = e / jnp.sum(e, axis=0, keepdims=True)
    lb = jnp.zeros_like(soft[0:1])
    for l in range(1, layer_idx + 1):
        lb = lb + soft[l:l + 1]

    K = HGRN_HEAD_DIM
    n_heads = q_f.shape[1] // K
    for d, (q_ref, f_ref, v_ref, o_ref) in enumerate(((q_f, f_f, v_f, of_ref), (q_b, f_b, v_b, ob_ref))):
        rev = d == 1
        Q = q_ref.shape[0]
        nsub = Q // sub
        r = lax.broadcasted_iota(jnp.int32, (Q, Q), 0)
        c = lax.broadcasted_iota(jnp.int32, (Q, Q), 1)
        mask = (c >= r) if rev else (c <= r)
        same = (r // sub) == (c // sub)
        tri = jnp.where(mask, 1.0, 0.0).astype(BF16)
        tri_blk = jnp.where(mask & same, 1.0, 0.0).astype(BF16)
        f = lb + (1.0 - lb) * _sigmoid(f_ref[...])
        kk = 1.0 - f
        lg = jnp.log(f)
        cs = _ones_dot(tri, lg)
        a = _ones_dot(tri_blk, lg)
        rr = cs - a
        tot = cs[Q - 1:Q, :] if not rev else cs[0:1, :]
        qv = q_ref[...]
        qt = (qv * jnp.exp(a)).astype(BF16)
        qe = (qv * jnp.exp(cs)).astype(BF16)
        kend = (kk * jnp.exp(tot - cs)).astype(BF16)
        etot = jnp.exp(tot)
        v_bf = v_ref[...].astype(BF16)
        outs = []
        for h in range(n_heads):
            hs = slice(h * K, (h + 1) * K)
            w_rows = []
            for i in range(nsub):
                r_i = rr[i * sub:i * sub + 1, hs]
                lo, hi = (i * sub, Q) if rev else (0, (i + 1) * sub)
                kt = (kk[lo:hi, hs] * jnp.exp(jnp.minimum(r_i - cs[lo:hi, hs], EXP_CLAMP))).astype(BF16)
                pads = [jnp.zeros((n, K), BF16) for n in (lo, Q - hi)]
                kt = jnp.concatenate([t for t in (pads[0], kt, pads[1]) if t.shape[0]], axis=0)
                w_rows.append(_dot_nt(qt[i * sub:(i + 1) * sub, hs], kt))
            w = jnp.where(mask, jnp.concatenate(w_rows, axis=0), 0.0).astype(BF16)
            st = st_ref[d, h]
            outs.append(_dot(w, v_bf[:, hs]) + _dot_nt(qe[:, hs], st.astype(BF16)))
            st_ref[d, h] = st * etot[:, hs] + _dot_tn(v_bf[:, hs], kend[:, hs])
        o_ref[...] = jnp.concatenate(outs, axis=1)


def _hgrn_scan(dims, layer_idx, lb_all, qa, ffw, fbw, va, q, sub):
    D = dims.D
    n_heads = D // HGRN_HEAD_DIM
    n_steps, fwd_pos, bwd_pos = _scan_positions(dims, q)
    fwd = lambda b, s: (fwd_pos(b, s), 0)
    bwd = lambda b, s: (bwd_pos(b, s), 0)
    st_shape = (2, n_heads, HGRN_HEAD_DIM, HGRN_HEAD_DIM)
    blk = lambda m: pl.BlockSpec((q, D), m)
    o_shape = jax.ShapeDtypeStruct((dims.T, D), F32)
    return pl.pallas_call(
        functools.partial(_hgrn_scan_kernel, layer_idx, sub),
        grid=(dims.B, n_steps),
        in_specs=[
            pl.BlockSpec(lb_all.shape, lambda b, s: (0, 0)),
            blk(fwd), blk(fwd), blk(fwd), blk(bwd), blk(bwd), blk(bwd),
        ],
        out_specs=(blk(fwd), blk(bwd)),
        out_shape=(o_shape, o_shape),
        scratch_shapes=[pltpu.VMEM(st_shape, F32)],
        compiler_params=_cparams("parallel", "arbitrary"),
        name="hgrn_scan",
    )(lb_all, qa, ffw, va, qa, fbw, va)


def _hgrn_finish(of_ref, ob_ref, g_ref, ng_ref):
    o = _group_rmsnorm(of_ref[...] + ob_ref[...], ng_ref[...], HGRN_HEAD_DIM)
    return o * _silu(g_ref[...])


def _hgrn_layer(dims, X, g, tabs, w_in, lb_all, layer_idx, norm_g, w_out, ctx_out):
    D = dims.D

    def epi(y, extra, outs):
        outs[0][...] = _silu(y[:, :D])
        for j in range(1, 5):
            outs[j][...] = y[:, j * D:(j + 1) * D]

    tm = 256
    row = lambda i: (i, 0)
    shp = jax.ShapeDtypeStruct((dims.T, D), F32)
    spec = pl.BlockSpec((tm, D), row)
    qa, ffw, fbw, va, ga = _norm_mod_proj(
        dims, X, g, tabs[0], tabs[1], w_in.astype(BF16), epi, (shp,) * 5, (spec,) * 5, tm=tm, name="hgrn_in")
    o_f, o_b = _hgrn_scan(dims, layer_idx, lb_all, qa, ffw, fbw, va, min(128, dims.C), 16)
    n_rows = dims.T if ctx_out else dims.TL
    return _out_proj(dims, n_rows, [o_f, o_b, ga, norm_g.reshape(1, D)],
                     [spec, spec, spec, pl.BlockSpec((1, D), lambda i: (0, 0))],
                     _hgrn_finish, w_out.astype(BF16), tabs[2], X, tm=tm, name="hgrn_out")


SUBLANES = 8
PACKED_ROWS = 16


def _batcher_pairs(n):
    pairs = []
    p = 1
    while p < n:
        k = p
        while k >= 1:
            for j in range(k % p, n - k, 2 * k):
                for i in range(min(k, n - j - k)):
                    if (i + j) // (2 * p) == (i + j + k) // (2 * p):
                        pairs.append((i + j, i + j + k))
            k //= 2
        p *= 2
    return pairs


def _bitonic_merge_desc(c):
    n = len(c)
    k = n // 2
    while k >= 1:
        for i in range(n):
            if not i & k:
                c[i], c[i + k] = jnp.maximum(c[i], c[i + k]), jnp.minimum(c[i], c[i + k])
        k //= 2
    return c


def _top16_sorted(s):
    k = PEER_TOPK
    x = [s[v * SUBLANES:(v + 1) * SUBLANES] for v in range(s.shape[0] // SUBLANES)]
    assert len(x) == k
    for i, j in _batcher_pairs(k):
        x[i], x[j] = jnp.maximum(x[i], x[j]), jnp.minimum(x[i], x[j])
    shift = SUBLANES // 2
    while shift >= 1:
        y = [pltpu.roll(t, shift, axis=0) for t in x]
        x = _bitonic_merge_desc([jnp.maximum(x[i], y[k - 1 - i]) for i in range(k)])
        shift //= 2
    return x


def _rank_rows(s, k):
    n = s.shape[0]
    rows = lax.broadcasted_iota(jnp.int32, s.shape, 0)
    rank = jnp.full(s.shape, float(k), F32)
    vals = []
    for r in range(k):
        m = jnp.max(s, axis=0, keepdims=True)
        vals.append(m)
        hit = rows == jnp.min(jnp.where(s == m, rows, n), axis=0, keepdims=True)
        rank = jnp.where(hit, float(r), rank)
        s = jnp.where(hit, -jnp.inf, s)
    return rank, vals


def _candidate_groups(sv1, sv2):
    tb = sv1[0].shape[1]
    sub = lax.broadcasted_iota(jnp.int32, (SUBLANES, tb), 0)

    def stack(rows8):
        out = jnp.broadcast_to(rows8[SUBLANES - 1], (SUBLANES, tb))
        for i in range(SUBLANES - 2, -1, -1):
            out = jnp.where(sub == i, rows8[i], out)
        return out

    lo2, hi2, hi1 = stack(sv2[:SUBLANES]), stack(sv2[SUBLANES:]), stack(sv1[SUBLANES:])
    return [sv1[0] + lo2, sv1[0] + hi2] + [sv1[a] + lo2 for a in range(1, SUBLANES)] + [hi1 + sv2[0]]


def _group_counts(picked):
    cnt = lambda t: jnp.sum(t, axis=0, keepdims=True)
    n_sel = [cnt(picked[0] + picked[1])] + [cnt(picked[a + 1]) for a in range(1, SUBLANES)]
    return n_sel + [picked[SUBLANES + 1][i:i + 1] for i in range(SUBLANES)]


def _staircase(sv1, sv2):
    k = PEER_TOPK
    groups = _candidate_groups(sv1, sv2)
    rank, best = _rank_rows(jnp.concatenate(groups, axis=0), k)
    picked = jnp.where(rank < k, 1.0, 0.0)
    picked = [picked[i * SUBLANES:(i + 1) * SUBLANES] for i in range(len(groups))]
    z = functools.reduce(jnp.add, [jnp.exp(b - best[0]) for b in best])
    return _group_counts(picked), 1.0 / z


def _staircase_sorted(sv1, sv2):
    k = PEER_TOPK
    groups = _candidate_groups(sv1, sv2)
    ng = len(groups)
    assert SUBLANES <= ng <= k
    x = list(groups)
    for i, j in _batcher_pairs(k):
        if j < ng:
            x[i], x[j] = jnp.maximum(x[i], x[j]), jnp.minimum(x[i], x[j])
    shift = SUBLANES // 2
    while shift >= 1:
        y = [pltpu.roll(t, shift, axis=0) for t in x]
        c = []
        for i in range(k):
            a = x[i] if i < len(x) else None
            b = y[k - 1 - i] if k - 1 - i < len(y) else None
            c.append(a if b is None else b if a is None else jnp.maximum(a, b))
        x = _bitonic_merge_desc(c)
        shift //= 2
    top = [t[0:1] for t in x]
    picked = [jnp.where(g >= top[k - 1], 1.0, 0.0) for g in groups]
    n_sel = _group_counts(picked)
    tie = jnp.abs(functools.reduce(jnp.add, n_sel) - float(k))
    z = functools.reduce(jnp.add, [jnp.exp(t - top[0]) for t in top])
    return n_sel, 1.0 / z, tie


def _peer_select(st1, st2, exact_ties):
    k = PEER_TOPK
    if exact_ties:
        rank1, sv1 = _rank_rows(st1, k)
        rank2, sv2 = _rank_rows(st2, k)
        tie = None
    else:
        sv1 = [t[0:1] for t in _top16_sorted(st1)]
        sv2 = [t[0:1] for t in _top16_sorted(st2)]
        rank2 = jnp.full(st2.shape, float(k), F32)
        for r in range(k - 1, -1, -1):
            rank2 = jnp.where(st2 >= sv2[r], float(r), rank2)
        dup = [jnp.where(sv[r] == sv[r + 1], 1.0, 0.0) for sv in (sv1, sv2) for r in range(k - 1)]
        over = [jnp.sum(jnp.where(st >= sv[k - 1], 1.0, 0.0), axis=0, keepdims=True) - k
                for st, sv in ((st1, sv1), (st2, sv2))]
        tie = functools.reduce(jnp.maximum, dup + over)
    if exact_ties:
        n_sel, inv_z = _staircase(sv1, sv2)
    else:
        n_sel, inv_z, tie2 = _staircase_sorted(sv1, sv2)
        tie = jnp.maximum(tie, tie2)
    nr = jnp.zeros(st1.shape, F32)
    if exact_ties:
        for a in range(k):
            nr = jnp.where(rank1 == float(a), n_sel[a], nr)
    else:
        for a in range(k - 1, -1, -1):
            nr = jnp.where(st1 >= sv1[a], n_sel[a], nr)
    p1 = jnp.exp(st1 - sv1[0]) * inv_z
    p2 = jnp.exp(st2 - sv2[0])
    return nr, p1, rank2, p2, tie


def _bf16_pair_words(x):
    hi = pltpu.bitcast(x.astype(BF16).astype(F32), jnp.uint32)
    return hi | (hi >> 16)


def _bf16_row_tile(ref, hh, base, ii, lanes):
    start = pl.multiple_of(base + (ii // SUBLANES) * SUBLANES, SUBLANES)
    rows8 = ref[hh, pl.ds(start, SUBLANES), lanes]
    words = jnp.broadcast_to(rows8[ii % SUBLANES:ii % SUBLANES + 1, :], rows8.shape)
    return pltpu.bitcast(words, BF16)


def _peer_pre_kernel(n_heads, x_ref, g_ref, sh_ref, sc_ref, wq_ref, keys_ref,
                     h_ref, nr_ref, p1_ref, r2_ref, p2_ref):
    h = _norm_mod(x_ref[...], g_ref[...], sh_ref[0], sc_ref[0]).astype(BF16)
    h_ref[...] = h
    q = _dot(h, wq_ref[...]).astype(BF16)
    dk = keys_ref.shape[2]
    fast = keys_ref.shape[1] == PEER_TOPK * SUBLANES

    def scores(hh, c):
        j = hh * 2 + c
        return _dot_nt(keys_ref[j], q[:, j * dk:(j + 1) * dk])

    def emit(hh, exact_ties):
        st1, st2 = scores(hh, 0), scores(hh, 1)
        ties = []
        for lo in range(0, st1.shape[1], LANES):
            cols = slice(lo, lo + LANES)
            nr, p1, rank2, p2, tie = _peer_select(st1[:, cols], st2[:, cols], exact_ties)
            nr_ref[hh, :, cols] = _bf16_pair_words(nr)
            p1_ref[hh, :, cols] = _bf16_pair_words(p1)
            r2_ref[hh, :, cols] = rank2.astype(BF16)
            p2_ref[hh, :, cols] = p2.astype(BF16)
            ties.append(tie)
        return None if exact_ties else functools.reduce(jnp.maximum, ties)

    if not fast:
        for hh in range(n_heads):
            emit(hh, True)
        return
    ties = [emit(hh, False) for hh in range(n_heads)]
    any_tie = jnp.max(functools.reduce(jnp.maximum, ties)) > 0.0

    @pl.when(any_tie)
    def _():
        for hh in range(n_heads):
            emit(hh, True)


def _peer_pre(dims, X, n_rows, g, sh_tab, sc_tab, wq, keys, tb):
    D = dims.D
    H, _, n_keys, dk = keys.shape
    keys2 = keys.reshape(H * 2, n_keys, dk).astype(BF16)
    tab_spec = pl.BlockSpec((1, 1, D), lambda i: (dims.mod_row(i, tb), 0, 0))
    s_spec = pl.BlockSpec((H, n_keys, tb), lambda i: (0, 0, i))
    s_f32 = jax.ShapeDtypeStruct((H, n_keys, n_rows), jnp.uint32)
    s_bf16 = jax.ShapeDtypeStruct((H, n_keys, n_rows), BF16)
    return pl.pallas_call(
        functools.partial(_peer_pre_kernel, H),
        grid=(n_rows // tb,),
        in_specs=[
            pl.BlockSpec((tb, D), lambda i: (i, 0)),
            pl.BlockSpec((1, D), lambda i: (0, 0)),
            tab_spec, tab_spec,
            pl.BlockSpec(wq.shape, lambda i: (0, 0)),
            pl.BlockSpec(keys2.shape, lambda i: (0, 0, 0)),
        ],
        out_specs=(pl.BlockSpec((tb, D), lambda i: (i, 0)), s_spec, s_spec, s_spec, s_spec),
        out_shape=(jax.ShapeDtypeStruct((n_rows, D), BF16), s_f32, s_f32, s_bf16, s_bf16),
        compiler_params=_cparams("parallel"),
        name="peer_pre",
    )(X, g.reshape(1, D), sh_tab, sc_tab, wq.astype(BF16), keys2)


PEER_SECOND_K = 256


def _peer_main_kernel(n_chunks, final, h_ref, nr_ref, p1_ref, r2_ref, p2_ref, u_ref, vt_ref, x_ref,
                      gate_ref, fg_ref, o_ref, act_ref, p_ref, acc_ref):
    e = pl.program_id(1)
    n_heads, n_keys, tb = r2_ref.shape
    ec = u_ref.shape[0]

    @pl.when(e == 0)
    def _():
        acc_ref[...] = jnp.zeros_like(acc_ref)

    per_piece = PEER_SECOND_K // n_keys
    groups = n_keys // PACKED_ROWS
    i1_base = e * (ec // n_keys)
    lane_chunk = min(tb, 2 * LANES)
    zero = jnp.zeros((PACKED_ROWS, lane_chunk), BF16)
    for ii in range(ec // n_keys):
        for lo in range(0, tb, lane_chunk):
            lanes = slice(lo, lo + lane_chunk)
            g = [None] * groups
            for hh in range(n_heads):
                nrow = _bf16_row_tile(nr_ref, hh, i1_base, ii, lanes)
                prow = _bf16_row_tile(p1_ref, hh, i1_base, ii, lanes)
                for k in range(groups):
                    rows = slice(k * PACKED_ROWS, (k + 1) * PACKED_ROWS)
                    w = jnp.where(r2_ref[hh, rows, lanes] < nrow, prow * p2_ref[hh, rows, lanes], zero)
                    g[k] = w if g[k] is None else g[k] + w
            for k in range(groups):
                rows = slice(ii * n_keys + k * PACKED_ROWS, ii * n_keys + (k + 1) * PACKED_ROWS)
                p_ref[rows, lanes] = g[k]
    act_ref[...] = _dot_nt(u_ref[...], h_ref[...])
    for r in range(0, ec, PACKED_ROWS):
        rows = slice(r, r + PACKED_ROWS)
        p_ref[rows, :] = p_ref[rows, :] * _gelu_exact(act_ref[rows, :]).astype(BF16)
    acc_ref[...] += _dot(vt_ref[...], p_ref[...])

    @pl.when(e == n_chunks - 1)
    def _():
        out = x_ref[...] + gate_ref[0] * acc_ref[...].T
        if final:
            out = out * lax.rsqrt(jnp.mean(out * out, axis=-1, keepdims=True) + EPS) * fg_ref[...]
        o_ref[...] = out


def _peer_layer(dims, X, n_rows, g, tabs, wq, keys, u, v, final_g):
    D = dims.D
    H, _, n_keys, _ = keys.shape
    n_exp = u.shape[0]
    h2, nr, p1, r2, p2 = _peer_pre(dims, X, n_rows, g, tabs[3], tabs[4], wq, keys, dims.tile(n_rows, 256))
    tb = dims.tile(n_rows, 512)
    ec = min(1024, n_exp)
    assert ec % PEER_SECOND_K == 0 and PEER_SECOND_K % n_keys == 0 and n_keys % PACKED_ROWS == 0
    assert (ec // n_keys) % SUBLANES == 0
    n_chunks = n_exp // ec
    u_bf = u.astype(BF16)
    vt_bf = v.astype(BF16).T
    final = final_g is not None
    fg = (final_g if final else jnp.ones((D,), F32)).reshape(1, D)
    s_spec = pl.BlockSpec((H, n_keys, tb), lambda i, e: (0, 0, i))
    return pl.pallas_call(
        functools.partial(_peer_main_kernel, n_chunks, final),
        grid=(n_rows // tb, n_chunks),
        in_specs=[
            pl.BlockSpec((tb, D), lambda i, e: (i, 0)),
            s_spec, s_spec, s_spec, s_spec,
            pl.BlockSpec((ec, D), lambda i, e: (e, 0)),
            pl.BlockSpec((D, ec), lambda i, e: (0, e)),
            pl.BlockSpec((tb, D), lambda i, e: (i, 0)),
            pl.BlockSpec((1, 1, D), lambda i, e: (dims.mod_row(i, tb), 0, 0)),
            pl.BlockSpec((1, D), lambda i, e: (0, 0)),
        ],
        out_specs=pl.BlockSpec((tb, D), lambda i, e: (i, 0)),
        out_shape=jax.ShapeDtypeStruct((n_rows, D), F32),
        scratch_shapes=[pltpu.VMEM((ec, tb), F32), pltpu.VMEM((ec, tb), BF16), pltpu.VMEM((D, tb), F32)],
        compiler_params=_cparams("parallel", "arbitrary"),
        name="peer_main",
    )(h2, nr, p1, r2, p2, u_bf, vt_bf, X, tabs[5], fg)


def kernel(x, c, ctx, c_ctx, w_mod, b_mod, norm_g, four_w_out, diff_w_in, diff_lam, diff_norm_g, diff_w_out,
           ssd_w_in, ssd_conv_w, ssd_conv_b, ssd_a_log, ssd_dt_bias, ssd_d, ssd_norm_g, ssd_w_out, hgrn_w_in,
           hgrn_lb, hgrn_norm_g, hgrn_w_out, peer_wq, peer_keys, peer_u, peer_v, final_g):
    B, L, D = x.shape
    C = ctx.shape[1]
    depth = w_mod.shape[0]
    n_mixers = 4
    dims = Dims(B, L, C, D)
    assert B + 1 <= MOD_ROWS

    X = jnp.concatenate([x.reshape(B * L, D), ctx.reshape(B * C, D)], axis=0)
    cc = jnp.concatenate([c, c_ctx[None, :], jnp.zeros((MOD_ROWS - B - 1, D), c.dtype)], axis=0)
    mods = _mod_tables(cc, w_mod, b_mod)

    for i in range(depth):
        last = i == depth - 1
        kind, j = i % n_mixers, i // n_mixers
        tabs = [mods[i, :, k * D:(k + 1) * D].reshape(MOD_ROWS, 1, D) for k in range(6)]
        g1, g2 = norm_g[i, 0], norm_g[i, 1]
        if kind == 0:
            X = _fourier_layer(dims, X, g1, tabs, four_w_out[j])
        elif kind == 1:
            X = _diff_layer(dims, X, g1, tabs, diff_w_in[j], diff_lam[j], diff_norm_g[j], diff_w_out[j], i)
        elif kind == 2:
            X = _ssd_layer(dims, X, g1, tabs, ssd_w_in[j], ssd_conv_w[j], ssd_conv_b[j], ssd_a_log[j],
                           ssd_dt_bias[j], ssd_d[j], ssd_norm_g[j], ssd_w_out[j])
        else:
            X = _hgrn_layer(dims, X, g1, tabs, hgrn_w_in[j], hgrn_lb, i, hgrn_norm_g[j], hgrn_w_out[j],
                            not last)
        n_rows = dims.TL if last else dims.T
        X = _peer_layer(dims, X, n_rows, g2, tabs, peer_wq[i], peer_keys[i], peer_u[i], peer_v[i],
                        final_g if last else None)
    return X[:B * L].reshape(B, L, D)
```

```python
import functools
import math

import jax
import jax.numpy as jnp
import numpy as np
from jax import lax
from jax.experimental import pallas as pl
from jax.experimental.pallas import tpu as pltpu

F32 = jnp.float32
BF16 = jnp.bfloat16

EPS = 1e-6
ROPE_BASE = 10000.0
GRID_W = 64
FOURIER_GROUPS = 8
DIFF_HEADS = 8
DIFF_HEAD_DIM = 64
SSD_HEAD_DIM = 64
SSD_STATE = 128
SSD_GROUPS = 8
HGRN_HEAD_DIM = 128
PEER_TOPK = 16

LANES = 128
VMEM_LIMIT_BYTES = 56 * 2**20
MOD_ROWS = 16
EXP_CLAMP = 80.0


def _cparams(*sem):
    return pltpu.CompilerParams(dimension_semantics=sem, vmem_limit_bytes=VMEM_LIMIT_BYTES)


class Dims:
    def __init__(self, B, L, C, D):
        self.B, self.L, self.C, self.D = B, L, C, D
        self.TL = B * L
        self.T = B * L + B * C

    def tile(self, n_rows, pref):
        t = pref
        while self.L % t or n_rows % t:
            t //= 2
        return t

    def mod_row(self, i, tm):
        n_lat = self.TL // tm
        return jnp.where(i < n_lat, i // (self.L // tm), self.B)


def _silu(x):
    return x * (1.0 / (1.0 + jnp.exp(-x)))


def _sigmoid(x):
    return 1.0 / (1.0 + jnp.exp(-x))


def _softplus(x):
    return jnp.maximum(x, 0.0) + jnp.log(1.0 + jnp.exp(-jnp.abs(x)))


def _gelu_exact(x):
    return 0.5 * x * (1.0 + lax.erf(x * (1.0 / math.sqrt(2.0))))


def _norm_mod(x, g, shift, scale):
    y = x * lax.rsqrt(jnp.mean(x * x, axis=-1, keepdims=True) + EPS)
    return (y * g) * (1.0 + scale) + shift


def _group_rmsnorm(y, g, group):
    parts = []
    for s in range(0, y.shape[1], group):
        p = y[:, s:s + group]
        parts.append(p * lax.rsqrt(jnp.mean(p * p, axis=-1, keepdims=True) + EPS))
    return jnp.concatenate(parts, axis=1) * g


def _split3(x):
    a = x.astype(BF16)
    r = x - a.astype(F32)
    b = r.astype(BF16)
    c = (r - b.astype(F32)).astype(BF16)
    return a, b, c


def _dot(a, b):
    return jnp.dot(a, b, preferred_element_type=F32)


def _dot_nt(a, b):
    return lax.dot_general(a, b, (((1,), (1,)), ((), ())), preferred_element_type=F32)


def _dot_tn(a, b):
    return lax.dot_general(a, b, (((0,), (0,)), ((), ())), preferred_element_type=F32)


def _ones_dot(mat01, x):
    a, b, c = _split3(x)
    return _dot(mat01, a) + _dot(mat01, b) + _dot(mat01, c)


def _mod_kernel(c_ref, w_ref, b_ref, o_ref):
    h = _silu(c_ref[...]).astype(BF16)
    o_ref[0] = _dot(h, w_ref[0].astype(BF16)) + b_ref[0]


def _mod_tables(cc, w_mod, b_mod):
    depth, D, N = w_mod.shape
    tn = 1536
    return pl.pallas_call(
        _mod_kernel,
        grid=(depth, N // tn),
        in_specs=[
            pl.BlockSpec((MOD_ROWS, D), lambda l, j: (0, 0)),
            pl.BlockSpec((1, D, tn), lambda l, j: (l, 0, j)),
            pl.BlockSpec((1, 1, tn), lambda l, j: (l, 0, j)),
        ],
        out_specs=pl.BlockSpec((1, MOD_ROWS, tn), lambda l, j: (l, 0, j)),
        out_shape=jax.ShapeDtypeStruct((depth, MOD_ROWS, N), F32),
        compiler_params=_cparams("parallel", "parallel"),
        name="mod_tables",
    )(cc, w_mod, b_mod.reshape(depth, 1, N))


def _proj_kernel(epilogue, n_extra, x_ref, g_ref, sh_ref, sc_ref, w_ref, *rest):
    h = _norm_mod(x_ref[...], g_ref[...], sh_ref[0], sc_ref[0])
    y = _dot(h.astype(BF16), w_ref[...])
    epilogue(y, rest[:n_extra], rest[n_extra:])


def _norm_mod_proj(dims, x, g, shift_tab, scale_tab, w, epilogue, out_shapes, out_specs,
                   extra=(), extra_specs=(), tm=256, name="proj"):
    D = dims.D
    n_rows = x.shape[0]
    tab_spec = pl.BlockSpec((1, 1, D), lambda i: (dims.mod_row(i, tm), 0, 0))
    return pl.pallas_call(
        functools.partial(_proj_kernel, epilogue, len(extra)),
        grid=(n_rows // tm,),
        in_specs=[
            pl.BlockSpec((tm, D), lambda i: (i, 0)),
            pl.BlockSpec((1, D), lambda i: (0, 0)),
            tab_spec, tab_spec,
            pl.BlockSpec(w.shape, lambda i: (0, 0)),
            *extra_specs,
        ],
        out_specs=out_specs,
        out_shape=out_shapes,
        compiler_params=_cparams("parallel"),
        name=name,
    )(x, g.reshape(1, D), shift_tab, scale_tab, w, *extra)


def _out_proj_kernel(prologue, n_in, *refs):
    ins = refs[:n_in]
    w_ref, gate_ref, x_ref, o_ref = refs[n_in:]
    y = prologue(*ins)
    o_ref[...] = x_ref[...] + gate_ref[0] * _dot(y.astype(BF16), w_ref[...])


def _out_proj(dims, n_rows, ins, in_specs, prologue, w, gate_tab, x, tm=256, name="out_proj"):
    D = dims.D
    return pl.pallas_call(
        functools.partial(_out_proj_kernel, prologue, len(ins)),
        grid=(n_rows // tm,),
        in_specs=[
            *in_specs,
            pl.BlockSpec(w.shape, lambda i: (0, 0)),
            pl.BlockSpec((1, 1, D), lambda i: (dims.mod_row(i, tm), 0, 0)),
            pl.BlockSpec((tm, D), lambda i: (i, 0)),
        ],
        out_specs=pl.BlockSpec((tm, D), lambda i: (i, 0)),
        out_shape=jax.ShapeDtypeStruct((n_rows, D), F32),
        compiler_params=_cparams("parallel"),
        name=name,
    )(*ins, w, gate_tab, x)


def _dft_tables(n):
    k = jnp.arange(n, dtype=jnp.int32)
    ang = (2.0 * math.pi / n) * ((k[:, None] * k[None, :]) % n).astype(F32)
    return jnp.cos(ang), jnp.sin(ang)


def _seq_dft_kernel(n_lat, scale_l, scale_c, cl_ref, sl_ref, hl_ref, cc_ref, sc_ref, hc_ref, o_ref):
    i = pl.program_id(1)
    n = hl_ref.shape[1] // 2

    @pl.when(i < n_lat)
    def _():
        acc = _dot(cl_ref[...], hl_ref[:, :n]) - _dot(sl_ref[...], hl_ref[:, n:])
        o_ref[...] = acc * scale_l

    @pl.when(i == n_lat)
    def _():
        acc = _dot(cc_ref[...], hc_ref[:, :n]) - _dot(sc_ref[...], hc_ref[:, n:])
        o_ref[...] = acc * scale_c


def _seq_dft(dims, hcs, group_width):
    D, L, C = dims.D, dims.L, dims.C
    assert L % C == 0
    n_lat = L // C
    cblk0 = dims.TL // C
    cl, sl = _dft_tables(L)
    cc, sc = _dft_tables(C)
    lat_blk = lambda b, i: (jnp.minimum(i, n_lat - 1), 0)
    return pl.pallas_call(
        functools.partial(_seq_dft_kernel, n_lat, 1.0 / math.sqrt(L * group_width),
                          1.0 / math.sqrt(C * group_width)),
        grid=(dims.B, n_lat + 1),
        in_specs=[
            pl.BlockSpec((C, L), lat_blk),
            pl.BlockSpec((C, L), lat_blk),
            pl.BlockSpec((L, 2 * D), lambda b, i: (b, 0)),
            pl.BlockSpec((C, C), lambda b, i: (0, 0)),
            pl.BlockSpec((C, C), lambda b, i: (0, 0)),
            pl.BlockSpec((C, 2 * D), lambda b, i: (cblk0 + b, 0)),
        ],
        out_specs=pl.BlockSpec((C, D), lambda b, i: (jnp.where(i < n_lat, b * n_lat + i, cblk0 + b), 0)),
        out_shape=jax.ShapeDtypeStruct((dims.T, D), F32),
        compiler_params=_cparams("parallel", "arbitrary"),
        name="seq_dft",
    )(cl.astype(BF16), sl.astype(BF16), hcs, cc.astype(BF16), sc.astype(BF16), hcs)


def _fourier_layer(dims, X, g, tabs, w_out):
    D = dims.D
    gw = D // FOURIER_GROUPS
    cg, sg = _dft_tables(gw)
    eye = jnp.eye(FOURIER_GROUPS, dtype=F32)
    wcs = jnp.concatenate([jnp.kron(eye, cg), jnp.kron(eye, sg)], axis=1).astype(BF16)

    def epi(y, extra, outs):
        outs[0][...] = y.astype(BF16)

    hcs = _norm_mod_proj(
        dims, X, g, tabs[0], tabs[1], wcs, epi,
        jax.ShapeDtypeStruct((dims.T, 2 * D), BF16),
        pl.BlockSpec((256, 2 * D), lambda i: (i, 0)), name="fourier_in")
    f = _seq_dft(dims, hcs, gw)
    return _out_proj(dims, dims.T, [f], [pl.BlockSpec((256, D), lambda i: (i, 0))],
                     lambda r: r[...], w_out.astype(BF16), tabs[2], X, name="fourier_out")


def _rope_tables(dims, tm):
    L, Dh = dims.L, DIFF_HEAD_DIM
    nf = Dh // 4
    pos = jnp.arange(L)
    row = (pos // GRID_W).astype(F32)
    col = (pos % GRID_W).astype(F32)
    freqs = ROPE_BASE ** (-jnp.arange(nf, dtype=F32) / nf)
    ar, ac = row[:, None] * freqs, col[:, None] * freqs
    cos_c = jnp.concatenate([jnp.cos(ar), jnp.cos(ar), jnp.cos(ac), jnp.cos(ac)], axis=1)
    sin_c = jnp.concatenate([-jnp.sin(ar), jnp.sin(ar), -jnp.sin(ac), jnp.sin(ac)], axis=1)
    cos_t = jnp.concatenate([cos_c, cos_c], axis=1)
    sin_t = jnp.concatenate([sin_c, sin_c], axis=1)
    cos_t = jnp.concatenate([cos_t, jnp.ones((tm, LANES), F32)], axis=0)
    sin_t = jnp.concatenate([sin_t, jnp.zeros((tm, LANES), F32)], axis=0)
    return cos_t, sin_t


def _rope_lanes(x, cos, sin):
    nf = DIFF_HEAD_DIM // 4
    lane = lax.broadcasted_iota(jnp.int32, x.shape, 1)
    partner = jnp.where((lane % (2 * nf)) < nf,
                        pltpu.roll(x, LANES - nf, axis=1), pltpu.roll(x, nf, axis=1))
    return x * cos + partner * sin


def _diff_qkv_epilogue(D, y, extra, outs):
    cos, sin = extra[0][...], extra[1][...]
    q_ref, k_ref, v_ref = outs
    qscale = DIFF_HEAD_DIM ** -0.5
    for h in range(DIFF_HEADS):
        sl = slice(h * LANES, (h + 1) * LANES)
        q_ref[h] = (_rope_lanes(y[:, sl], cos, sin) * qscale).astype(BF16)
        k_ref[h] = _rope_lanes(y[:, D + h * LANES:D + (h + 1) * LANES], cos, sin).astype(BF16)
        v_ref[h] = y[:, 2 * D + h * LANES:2 * D + (h + 1) * LANES].astype(BF16)


def _diff_attn_rows(lam_init, lam_ref, subg_ref, q_ref, kv, o_ref):
    lam = lam_ref[...]
    lam_full = (jnp.exp(jnp.sum(lam[0:1] * lam[1:2], axis=-1, keepdims=True))
                - jnp.exp(jnp.sum(lam[2:3] * lam[3:4], axis=-1, keepdims=True)) + lam_init)
    q = q_ref[0]
    lane = lax.broadcasted_iota(jnp.int32, q.shape, 1)
    zero = jnp.zeros_like(q)
    probs = []
    for c in range(2):
        qc = jnp.where((lane >= c * DIFF_HEAD_DIM) & (lane < (c + 1) * DIFF_HEAD_DIM), q, zero)
        s = [_dot_nt(qc, k_ref[0]) for k_ref, _ in kv]
        m = functools.reduce(jnp.maximum, [jnp.max(t, axis=-1, keepdims=True) for t in s])
        e = [jnp.exp(t - m) for t in s]
        inv = 1.0 / functools.reduce(jnp.add, [jnp.sum(t, axis=-1, keepdims=True) for t in e])
        probs.append([t * inv for t in e])
    o = None
    for j, (_, v_ref) in enumerate(kv):
        a = (probs[0][j] - lam_full * probs[1][j]).astype(BF16)
        t = _dot(a, v_ref[0])
        o = t if o is None else o + t
    o = o * lax.rsqrt(jnp.mean(o * o, axis=-1, keepdims=True) + EPS)
    o_ref[0] = o * subg_ref[...] * (1.0 - lam_init)


def _diff_attn_kernel(n_lat, lam_init, lam_ref, subg_ref, q_ref, kl_ref, vl_ref, kc_ref, vc_ref, o_ref):
    i = pl.program_id(2)

    @pl.when(i < n_lat)
    def _():
        _diff_attn_rows(lam_init, lam_ref, subg_ref, q_ref, [(kl_ref, vl_ref), (kc_ref, vc_ref)], o_ref)

    @pl.when(i == n_lat)
    def _():
        _diff_attn_rows(lam_init, lam_ref, subg_ref, q_ref, [(kc_ref, vc_ref)], o_ref)


def _diff_attention(dims, q, k, v, lam, subg, lam_init):
    H = DIFF_HEADS
    B, L, C = dims.B, dims.L, dims.C
    assert L % C == 0
    n_lat = L // C
    cblk0 = dims.TL // C
    q_map = lambda b, h, i: (h, jnp.where(i < n_lat, b * n_lat + i, cblk0 + b), 0)
    lat = pl.BlockSpec((1, L, LANES), lambda b, h, i: (h, b, 0))
    ctx = pl.BlockSpec((1, C, LANES), lambda b, h, i: (h, cblk0 + b, 0))
    return pl.pallas_call(
        functools.partial(_diff_attn_kernel, n_lat, lam_init),
        grid=(B, H, n_lat + 1),
        in_specs=[
            pl.BlockSpec(lam.shape, lambda b, h, i: (0, 0)),
            pl.BlockSpec((1, LANES), lambda b, h, i: (0, 0)),
            pl.BlockSpec((1, C, LANES), q_map),
            lat, lat, ctx, ctx,
        ],
        out_specs=pl.BlockSpec((1, C, LANES), q_map),
        out_shape=jax.ShapeDtypeStruct((H, dims.T, LANES), F32),
        compiler_params=_cparams("parallel", "parallel", "arbitrary"),
        name="diff_attn",
    )(lam, subg.reshape(1, LANES), q, k, v, k, v)


def _heads_to_lanes(ref):
    return jnp.concatenate([ref[h] for h in range(ref.shape[0])], axis=1)


def _diff_layer(dims, X, g, tabs, w_in, lam, subg, w_out, depth_idx):
    D, H = dims.D, DIFF_HEADS
    tm = 256
    cos_t, sin_t = _rope_tables(dims, tm)
    n_lat, bpb = dims.TL // tm, dims.L // tm
    rope_spec = pl.BlockSpec((tm, LANES), lambda i: (jnp.where(i < n_lat, i % bpb, bpb), 0))
    hm = jax.ShapeDtypeStruct((H, dims.T, LANES), BF16)
    hm_spec = pl.BlockSpec((H, tm, LANES), lambda i: (0, i, 0))
    q, k, v = _norm_mod_proj(
        dims, X, g, tabs[0], tabs[1], w_in.astype(BF16), functools.partial(_diff_qkv_epilogue, D),
        (hm, hm, hm), (hm_spec, hm_spec, hm_spec),
        extra=(cos_t, sin_t), extra_specs=(rope_spec, rope_spec), tm=tm, name="diff_in")
    lam_init = 0.8 - 0.6 * math.exp(-0.3 * depth_idx)
    o = _diff_attention(dims, q, k, v, lam, subg, lam_init)
    return _out_proj(dims, dims.T, [o], [pl.BlockSpec((H, 256, LANES), lambda i: (0, i, 0))],
                     _heads_to_lanes, w_out.astype(BF16), tabs[2], X, name="diff_out")


def _conv_silu_kernel(width, n_lat, x_ref, prev_ref, next_ref, w_ref, b_ref, o_ref):
    r = pl.program_id(2)
    x = x_ref[...]
    rb = x.shape[0]
    pad = (width - 1) // 2
    assert pad <= SUBLANES
    before = jnp.where((r > 0) & (r < n_lat), prev_ref[...], 0.0)
    after = jnp.where(r < n_lat - 1, next_ref[...], 0.0)
    xp = jnp.concatenate([before, x, after], axis=0)
    n = xp.shape[0]
    acc = jnp.zeros_like(x) + b_ref[...]
    for t in range(width):
        off = t - pad
        xs = xp if off == 0 else pltpu.roll(xp, (-off) % n, axis=0)
        acc = acc + xs[SUBLANES:SUBLANES + rb] * w_ref[t:t + 1, :]
    o_ref[...] = _silu(acc)


def _conv_silu(dims, xbc, cw, cb):
    n_ch = xbc.shape[1]
    L, C = dims.L, dims.C
    assert L % C == 0 and C % SUBLANES == 0
    tc = min(2048, n_ch)
    n_lat = L // C
    cblk0 = dims.TL // C
    per8 = C // SUBLANES
    last8 = dims.T // SUBLANES - 1
    width = cw.shape[0]
    blk = lambda b, r: jnp.where(r < n_lat, b * n_lat + r, cblk0 + b)
    return pl.pallas_call(
        functools.partial(_conv_silu_kernel, width, n_lat),
        grid=(dims.B, n_ch // tc, n_lat + 1),
        in_specs=[
            pl.BlockSpec((C, tc), lambda b, j, r: (blk(b, r), j)),
            pl.BlockSpec((SUBLANES, tc), lambda b, j, r: (jnp.maximum(blk(b, r) * per8 - 1, 0), j)),
            pl.BlockSpec((SUBLANES, tc), lambda b, j, r: (jnp.minimum((blk(b, r) + 1) * per8, last8), j)),
            pl.BlockSpec((width, tc), lambda b, j, r: (0, j)),
            pl.BlockSpec((1, tc), lambda b, j, r: (0, j)),
        ],
        out_specs=pl.BlockSpec((C, tc), lambda b, j, r: (blk(b, r), j)),
        out_shape=jax.ShapeDtypeStruct(xbc.shape, F32),
        compiler_params=_cparams("parallel", "parallel", "arbitrary"),
        name="ssd_conv",
    )(xbc, xbc, xbc, cw, cb.reshape(1, n_ch))


def _tri(n, reverse):
    r = lax.broadcasted_iota(jnp.int32, (n, n), 0)
    c = lax.broadcasted_iota(jnp.int32, (n, n), 1)
    return (c >= r) if reverse else (c <= r)


def _ssd_scan_kernel(n_heads, a_ref, bias_ref, expand_ref,
                     xs_f, bm_f, cm_f, dt_f, xs_b, bm_b, cm_b, dt_b,
                     of_ref, ob_ref, st_ref):
    s = pl.program_id(1)

    @pl.when(s == 0)
    def _():
        st_ref[...] = jnp.zeros_like(st_ref)

    G = SSD_GROUPS
    N = SSD_STATE
    P = SSD_HEAD_DIM
    hpg = n_heads // G
    expand = expand_ref[...]
    for d, (xs_ref, bm_ref, cm_ref, dt_ref, o_ref) in enumerate(
            ((xs_f, bm_f, cm_f, dt_f, of_ref), (xs_b, bm_b, cm_b, dt_b, ob_ref))):
        rev = d == 1
        Q = xs_ref.shape[0]
        mask = _tri(Q, rev)
        tri = jnp.where(mask, 1.0, 0.0).astype(BF16)
        lane = lax.broadcasted_iota(jnp.int32, (Q, LANES), 1)
        mine = (lane >= d * n_heads) & (lane < (d + 1) * n_heads)
        dt = jnp.where(mine, _softplus(dt_ref[...] + bias_ref[...]), 0.0)
        if d == 1:
            dt = pltpu.roll(dt, LANES - n_heads, axis=1)
        la = dt * a_ref[d:d + 1, :]
        cs = _ones_dot(tri, la)
        tot = cs[Q - 1:Q, :] if not rev else cs[0:1, :]
        csT = cs.T
        dtT = dt.T
        ecs = jnp.exp(cs)
        wend = dt * jnp.exp(tot - cs)
        ecs_x = _dot(ecs.astype(BF16), expand) + _dot((ecs - ecs.astype(BF16).astype(F32)).astype(BF16), expand)
        wend_x = _dot(wend.astype(BF16), expand) + _dot((wend - wend.astype(BF16).astype(F32)).astype(BF16), expand)
        etot = jnp.exp(tot)
        etot_x = _dot(etot.astype(BF16), expand) + _dot((etot - etot.astype(BF16).astype(F32)).astype(BF16), expand)
        xs = xs_ref[...]
        xw = (xs * wend_x).astype(BF16)
        xs_bf = xs.astype(BF16)
        outs = []
        for g in range(G):
            cg = cm_ref[:, g * N:(g + 1) * N].astype(BF16)
            bg = bm_ref[:, g * N:(g + 1) * N].astype(BF16)
            cb = _dot_nt(cg, bg)
            gs = slice(g * hpg * P, (g + 1) * hpg * P)
            st = st_ref[d, g]
            inter = _dot(cg, st.astype(BF16)) * ecs_x[:, gs]
            intra = []
            for hh in range(hpg):
                h = g * hpg + hh
                seg = cs[:, h:h + 1] - csT[h:h + 1, :]
                m = cb * jnp.exp(jnp.where(mask, seg, -jnp.inf)) * dtT[h:h + 1, :]
                intra.append(_dot(m.astype(BF16), xs_bf[:, h * P:(h + 1) * P]))
            outs.append(inter + jnp.concatenate(intra, axis=1))
            st_ref[d, g] = st * etot_x[:, gs] + _dot_tn(bg, xw[:, gs])
        o_ref[...] = jnp.concatenate(outs, axis=1)


def _scan_positions(dims, q):
    ncc, ncl = dims.C // q, dims.L // q
    cblk0 = dims.TL // q
    fwd = lambda b, s: jnp.where(s < ncc, cblk0 + b * ncc + s, b * ncl + (s - ncc))
    bwd = lambda b, s: jnp.where(s < ncc, cblk0 + b * ncc + (ncc - 1 - s), b * ncl + (ncl - 1 - (s - ncc)))
    return ncc + ncl, fwd, bwd


def _ssd_scan(dims, xbc_act, dt, a, bias, q):
    d_inner = xbc_act.shape[1] - 2 * SSD_GROUPS * SSD_STATE
    n_heads = d_inner // SSD_HEAD_DIM
    G, N = SSD_GROUPS, SSD_STATE
    gn = G * N
    n_steps, fwd, bwd = _scan_positions(dims, q)
    xs_c, bm_c, cm_c = 0, d_inner // gn, d_inner // gn + 1

    def stream(pos):
        return [
            pl.BlockSpec((q, d_inner), lambda b, s: (pos(b, s), xs_c)),
            pl.BlockSpec((q, gn), lambda b, s: (pos(b, s), bm_c)),
            pl.BlockSpec((q, gn), lambda b, s: (pos(b, s), cm_c)),
            pl.BlockSpec((q, LANES), lambda b, s: (pos(b, s), 0)),
        ]

    expand = (jnp.arange(LANES)[:, None] == (jnp.arange(d_inner)[None, :] // SSD_HEAD_DIM)).astype(BF16)
    a_pad = jnp.zeros((2, LANES), F32).at[:, :n_heads].set(a)
    bias_pad = jnp.zeros((1, LANES), F32).at[0, :2 * n_heads].set(bias.reshape(-1))
    st_shape = (2, G, N, d_inner // G)
    in_specs = [
        pl.BlockSpec((2, LANES), lambda b, s: (0, 0)),
        pl.BlockSpec((1, LANES), lambda b, s: (0, 0)),
        pl.BlockSpec(expand.shape, lambda b, s: (0, 0)),
        *stream(fwd), *stream(bwd),
    ]
    args = [a_pad, bias_pad, expand, xbc_act, xbc_act, xbc_act, dt, xbc_act, xbc_act, xbc_act, dt]
    o_shape = jax.ShapeDtypeStruct((dims.T, d_inner), F32)
    return pl.pallas_call(
        functools.partial(_ssd_scan_kernel, n_heads),
        grid=(dims.B, n_steps),
        in_specs=in_specs,
        out_specs=(
            pl.BlockSpec((q, d_inner), lambda b, s: (fwd(b, s), 0)),
            pl.BlockSpec((q, d_inner), lambda b, s: (bwd(b, s), 0)),
        ),
        out_shape=(o_shape, o_shape),
        scratch_shapes=[pltpu.VMEM(st_shape, F32)],
        compiler_params=_cparams("parallel", "arbitrary"),
        name="ssd_scan",
    )(*args)


def _ssd_finish(group, of_ref, ob_ref, xs_ref, z_ref, dskip_ref, ng_ref):
    y = of_ref[...] + ob_ref[...] + xs_ref[...] * dskip_ref[...]
    return _group_rmsnorm(y * _silu(z_ref[...]), ng_ref[...], group)


def _ssd_layer(dims, X, g, tabs, w_in, conv_w, conv_b, a_log, dt_bias, d_skip, norm_g, w_out):
    D = dims.D
    d_inner = w_out.shape[0]
    n_heads = d_inner // SSD_HEAD_DIM
    conv_dim = conv_w.shape[1]
    n_dt = 2 * n_heads
    w_pad = jnp.concatenate([w_in, jnp.zeros((D, LANES - n_dt), w_in.dtype)], axis=1).astype(BF16)

    def epi(y, extra, outs):
        outs[0][...] = y[:, :d_inner]
        outs[1][...] = y[:, d_inner:d_inner + conv_dim]
        outs[2][...] = y[:, d_inner + conv_dim:]

    tm = 256
    row = lambda i: (i, 0)
    z, xbc, dt = _norm_mod_proj(
        dims, X, g, tabs[0], tabs[1], w_pad, epi,
        (jax.ShapeDtypeStruct((dims.T, d_inner), F32), jax.ShapeDtypeStruct((dims.T, conv_dim), F32),
         jax.ShapeDtypeStruct((dims.T, LANES), F32)),
        (pl.BlockSpec((tm, d_inner), row), pl.BlockSpec((tm, conv_dim), row), pl.BlockSpec((tm, LANES), row)),
        tm=tm, name="ssd_in")
    act = _conv_silu(dims, xbc, conv_w, conv_b)
    a = -jnp.exp(a_log.astype(F32))
    o_f, o_b = _ssd_scan(dims, act, dt, a, dt_bias, min(128, dims.C))
    dskip_x = jnp.repeat(d_skip.astype(F32), SSD_HEAD_DIM).reshape(1, d_inner)
    wide = pl.BlockSpec((tm, d_inner), row)
    vec = pl.BlockSpec((1, d_inner), lambda i: (0, 0))
    return _out_proj(
        dims, dims.T, [o_f, o_b, act, z, dskip_x, norm_g.reshape(1, d_inner)],
        [wide, wide, wide, wide, vec, vec],
        functools.partial(_ssd_finish, d_inner // SSD_GROUPS), w_out.astype(BF16), tabs[2], X, tm=tm,
        name="ssd_out")


def _hgrn_scan_kernel(layer_idx, sub, lb_ref, q_f, f_f, v_f, q_b, f_b, v_b, of_ref, ob_ref, st_ref):
    s = pl.program_id(1)

    @pl.when(s == 0)
    def _():
        st_ref[...] = jnp.zeros_like(st_ref)

    lbr = lb_ref[...]
    e = jnp.exp(lbr - jnp.max(lbr, axis=0, keepdims=True))
    soft = e / jnp.sum(e, axis=0, keepdims=True)
    lb = jnp.zeros_like(soft[0:1])
    for l in range(1, layer_idx + 1):
        lb = lb + soft[l:l + 1]

    K = HGRN_HEAD_DIM
    n_heads = q_f.shape[1] // K
    for d, (q_ref, f_ref, v_ref, o_ref) in enumerate(((q_f, f_f, v_f, of_ref), (q_b, f_b, v_b, ob_ref))):
        rev = d == 1
        Q = q_ref.shape[0]
        nsub = Q // sub
        r = lax.broadcasted_iota(jnp.int32, (Q, Q), 0)
        c = lax.broadcasted_iota(jnp.int32, (Q, Q), 1)
        mask = (c >= r) if rev else (c <= r)
        same = (r // sub) == (c // sub)
        tri = jnp.where(mask, 1.0, 0.0).astype(BF16)
        tri_blk = jnp.where(mask & same, 1.0, 0.0).astype(BF16)
        f = lb + (1.0 - lb) * _sigmoid(f_ref[...])
        kk = 1.0 - f
        lg = jnp.log(f)
        cs = _ones_dot(tri, lg)
        a = _ones_dot(tri_blk, lg)
        rr = cs - a
        tot = cs[Q - 1:Q, :] if not rev else cs[0:1, :]
        qv = q_ref[...]
        qt = (qv * jnp.exp(a)).astype(BF16)
        qe = (qv * jnp.exp(cs)).astype(BF16)
        kend = (kk * jnp.exp(tot - cs)).astype(BF16)
        etot = jnp.exp(tot)
        v_bf = v_ref[...].astype(BF16)
        outs = []
        for h in range(n_heads):
            hs = slice(h * K, (h + 1) * K)
            w_rows = []
            for i in range(nsub):
                r_i = rr[i * sub:i * sub + 1, hs]
                lo, hi = (i * sub, Q) if rev else (0, (i + 1) * sub)
                kt = (kk[lo:hi, hs] * jnp.exp(jnp.minimum(r_i - cs[lo:hi, hs], EXP_CLAMP))).astype(BF16)
                pads = [jnp.zeros((n, K), BF16) for n in (lo, Q - hi)]
                kt = jnp.concatenate([t for t in (pads[0], kt, pads[1]) if t.shape[0]], axis=0)
                w_rows.append(_dot_nt(qt[i * sub:(i + 1) * sub, hs], kt))
            w = jnp.where(mask, jnp.concatenate(w_rows, axis=0), 0.0).astype(BF16)
            st = st_ref[d, h]
            outs.append(_dot(w, v_bf[:, hs]) + _dot_nt(qe[:, hs], st.astype(BF16)))
            st_ref[d, h] = st * etot[:, hs] + _dot_tn(v_bf[:, hs], kend[:, hs])
        o_ref[...] = jnp.concatenate(outs, axis=1)


def _hgrn_scan(dims, layer_idx, lb_all, qa, ffw, fbw, va, q, sub):
    D = dims.D
    n_heads = D // HGRN_HEAD_DIM
    n_steps, fwd_pos, bwd_pos = _scan_positions(dims, q)
    fwd = lambda b, s: (fwd_pos(b, s), 0)
    bwd = lambda b, s: (bwd_pos(b, s), 0)
    st_shape = (2, n_heads, HGRN_HEAD_DIM, HGRN_HEAD_DIM)
    blk = lambda m: pl.BlockSpec((q, D), m)
    o_shape = jax.ShapeDtypeStruct((dims.T, D), F32)
    return pl.pallas_call(
        functools.partial(_hgrn_scan_kernel, layer_idx, sub),
        grid=(dims.B, n_steps),
        in_specs=[
            pl.BlockSpec(lb_all.shape, lambda b, s: (0, 0)),
            blk(fwd), blk(fwd), blk(fwd), blk(bwd), blk(bwd), blk(bwd),
        ],
        out_specs=(blk(fwd), blk(bwd)),
        out_shape=(o_shape, o_shape),
        scratch_shapes=[pltpu.VMEM(st_shape, F32)],
        compiler_params=_cparams("parallel", "arbitrary"),
        name="hgrn_scan",
    )(lb_all, qa, ffw, va, qa, fbw, va)


def _hgrn_finish(of_ref, ob_ref, g_ref, ng_ref):
    o = _group_rmsnorm(of_ref[...] + ob_ref[...], ng_ref[...], HGRN_HEAD_DIM)
    return o * _silu(g_ref[...])


def _hgrn_layer(dims, X, g, tabs, w_in, lb_all, layer_idx, norm_g, w_out, ctx_out):
    D = dims.D

    def epi(y, extra, outs):
        outs[0][...] = _silu(y[:, :D])
        for j in range(1, 5):
            outs[j][...] = y[:, j * D:(j + 1) * D]

    tm = 256
    row = lambda i: (i, 0)
    shp = jax.ShapeDtypeStruct((dims.T, D), F32)
    spec = pl.BlockSpec((tm, D), row)
    qa, ffw, fbw, va, ga = _norm_mod_proj(
        dims, X, g, tabs[0], tabs[1], w_in.astype(BF16), epi, (shp,) * 5, (spec,) * 5, tm=tm, name="hgrn_in")
    o_f, o_b = _hgrn_scan(dims, layer_idx, lb_all, qa, ffw, fbw, va, min(128, dims.C), 16)
    n_rows = dims.T if ctx_out else dims.TL
    return _out_proj(dims, n_rows, [o_f, o_b, ga, norm_g.reshape(1, D)],
                     [spec, spec, spec, pl.BlockSpec((1, D), lambda i: (0, 0))],
                     _hgrn_finish, w_out.astype(BF16), tabs[2], X, tm=tm, name="hgrn_out")


SUBLANES = 8
PACKED_ROWS = 16


def _batcher_pairs(n):
    pairs = []
    p = 1
    while p < n:
        k = p
        while k >= 1:
            for j in range(k % p, n - k, 2 * k):
                for i in range(min(k, n - j - k)):
                    if (i + j) // (2 * p) == (i + j + k) // (2 * p):
                        pairs.append((i + j, i + j + k))
            k //= 2
        p *= 2
    return pairs


def _bitonic_merge_desc(c):
    n = len(c)
    k = n // 2
    while k >= 1:
        for i in range(n):
            if not i & k:
                c[i], c[i + k] = jnp.maximum(c[i], c[i + k]), jnp.minimum(c[i], c[i + k])
        k //= 2
    return c


def _top16_sorted(s):
    k = PEER_TOPK
    x = [s[v * SUBLANES:(v + 1) * SUBLANES] for v in range(s.shape[0] // SUBLANES)]
    assert len(x) == k
    for i, j in _batcher_pairs(k):
        x[i], x[j] = jnp.maximum(x[i], x[j]), jnp.minimum(x[i], x[j])
    shift = SUBLANES // 2
    while shift >= 1:
        y = [pltpu.roll(t, shift, axis=0) for t in x]
        x = _bitonic_merge_desc([jnp.maximum(x[i], y[k - 1 - i]) for i in range(k)])
        shift //= 2
    return x


def _rank_rows(s, k):
    n = s.shape[0]
    rows = lax.broadcasted_iota(jnp.int32, s.shape, 0)
    rank = jnp.full(s.shape, float(k), F32)
    vals = []
    for r in range(k):
        m = jnp.max(s, axis=0, keepdims=True)
        vals.append(m)
        hit = rows == jnp.min(jnp.where(s == m, rows, n), axis=0, keepdims=True)
        rank = jnp.where(hit, float(r), rank)
        s = jnp.where(hit, -jnp.inf, s)
    return rank, vals


def _candidate_groups(sv1, sv2):
    tb = sv1[0].shape[1]
    sub = lax.broadcasted_iota(jnp.int32, (SUBLANES, tb), 0)

    def stack(rows8):
        out = jnp.broadcast_to(rows8[SUBLANES - 1], (SUBLANES, tb))
        for i in range(SUBLANES - 2, -1, -1):
            out = jnp.where(sub == i, rows8[i], out)
        return out

    lo2, hi2, hi1 = stack(sv2[:SUBLANES]), stack(sv2[SUBLANES:]), stack(sv1[SUBLANES:])
    return [sv1[0] + lo2, sv1[0] + hi2] + [sv1[a] + lo2 for a in range(1, SUBLANES)] + [hi1 + sv2[0]]


def _group_counts(picked):
    cnt = lambda t: jnp.sum(t, axis=0, keepdims=True)
    n_sel = [cnt(picked[0] + picked[1])] + [cnt(picked[a + 1]) for a in range(1, SUBLANES)]
    return n_sel + [picked[SUBLANES + 1][i:i + 1] for i in range(SUBLANES)]


def _staircase(sv1, sv2):
    k = PEER_TOPK
    groups = _candidate_groups(sv1, sv2)
    rank, best = _rank_rows(jnp.concatenate(groups, axis=0), k)
    picked = jnp.where(rank < k, 1.0, 0.0)
    picked = [picked[i * SUBLANES:(i + 1) * SUBLANES] for i in range(len(groups))]
    z = functools.reduce(jnp.add, [jnp.exp(b - best[0]) for b in best])
    return _group_counts(picked), 1.0 / z


def _staircase_sorted(sv1, sv2):
    k = PEER_TOPK
    groups = _candidate_groups(sv1, sv2)
    ng = len(groups)
    assert SUBLANES <= ng <= k
    x = list(groups)
    for i, j in _batcher_pairs(k):
        if j < ng:
            x[i], x[j] = jnp.maximum(x[i], x[j]), jnp.minimum(x[i], x[j])
    shift = SUBLANES // 2
    while shift >= 1:
        y = [pltpu.roll(t, shift, axis=0) for t in x]
        c = []
        for i in range(k):
            a = x[i] if i < len(x) else None
            b = y[k - 1 - i] if k - 1 - i < len(y) else None
            c.append(a if b is None else b if a is None else jnp.maximum(a, b))
        x = _bitonic_merge_desc(c)
        shift //= 2
    top = [t[0:1] for t in x]
    picked = [jnp.where(g >= top[k - 1], 1.0, 0.0) for g in groups]
    n_sel = _group_counts(picked)
    tie = jnp.abs(functools.reduce(jnp.add, n_sel) - float(k))
    z = functools.reduce(jnp.add, [jnp.exp(t - top[0]) for t in top])
    return n_sel, 1.0 / z, tie


def _peer_select(st1, st2, exact_ties):
    k = PEER_TOPK
    if exact_ties:
        rank1, sv1 = _rank_rows(st1, k)
        rank2, sv2 = _rank_rows(st2, k)
        tie = None
    else:
        sv1 = [t[0:1] for t in _top16_sorted(st1)]
        sv2 = [t[0:1] for t in _top16_sorted(st2)]
        rank2 = jnp.full(st2.shape, float(k), F32)
        for r in range(k - 1, -1, -1):
            rank2 = jnp.where(st2 >= sv2[r], float(r), rank2)
        dup = [jnp.where(sv[r] == sv[r + 1], 1.0, 0.0) for sv in (sv1, sv2) for r in range(k - 1)]
        over = [jnp.sum(jnp.where(st >= sv[k - 1], 1.0, 0.0), axis=0, keepdims=True) - k
                for st, sv in ((st1, sv1), (st2, sv2))]
        tie = functools.reduce(jnp.maximum, dup + over)
    if exact_ties:
        n_sel, inv_z = _staircase(sv1, sv2)
    else:
        n_sel, inv_z, tie2 = _staircase_sorted(sv1, sv2)
        tie = jnp.maximum(tie, tie2)
    nr = jnp.zeros(st1.shape, F32)
    if exact_ties:
        for a in range(k):
            nr = jnp.where(rank1 == float(a), n_sel[a], nr)
    else:
        for a in range(k - 1, -1, -1):
            nr = jnp.where(st1 >= sv1[a], n_sel[a], nr)
    p1 = jnp.exp(st1 - sv1[0]) * inv_z
    p2 = jnp.exp(st2 - sv2[0])
    return nr, p1, rank2, p2, tie


def _bf16_pair_words(x):
    hi = pltpu.bitcast(x.astype(BF16).astype(F32), jnp.uint32)
    return hi | (hi >> 16)


def _bf16_row_tile(ref, hh, base, ii, lanes):
    start = pl.multiple_of(base + (ii // SUBLANES) * SUBLANES, SUBLANES)
    rows8 = ref[hh, pl.ds(start, SUBLANES), lanes]
    words = jnp.broadcast_to(rows8[ii % SUBLANES:ii % SUBLANES + 1, :], rows8.shape)
    return pltpu.bitcast(words, BF16)


def _peer_pre_kernel(n_heads, x_ref, g_ref, sh_ref, sc_ref, wq_ref, keys_ref,
                     h_ref, nr_ref, p1_ref, r2_ref, p2_ref):
    h = _norm_mod(x_ref[...], g_ref[...], sh_ref[0], sc_ref[0]).astype(BF16)
    h_ref[...] = h
    q = _dot(h, wq_ref[...]).astype(BF16)
    dk = keys_ref.shape[2]
    fast = keys_ref.shape[1] == PEER_TOPK * SUBLANES

    def scores(hh, c):
        j = hh * 2 + c
        return _dot_nt(keys_ref[j], q[:, j * dk:(j + 1) * dk])

    def emit(hh, exact_ties):
        st1, st2 = scores(hh, 0), scores(hh, 1)
        ties = []
        for lo in range(0, st1.shape[1], LANES):
            cols = slice(lo, lo + LANES)
            nr, p1, rank2, p2, tie = _peer_select(st1[:, cols], st2[:, cols], exact_ties)
            nr_ref[hh, :, cols] = _bf16_pair_words(nr)
            p1_ref[hh, :, cols] = _bf16_pair_words(p1)
            r2_ref[hh, :, cols] = rank2.astype(BF16)
            p2_ref[hh, :, cols] = p2.astype(BF16)
            ties.append(tie)
        return None if exact_ties else functools.reduce(jnp.maximum, ties)

    if not fast:
        for hh in range(n_heads):
            emit(hh, True)
        return
    ties = [emit(hh, False) for hh in range(n_heads)]
    any_tie = jnp.max(functools.reduce(jnp.maximum, ties)) > 0.0

    @pl.when(any_tie)
    def _():
        for hh in range(n_heads):
            emit(hh, True)


def _peer_pre(dims, X, n_rows, g, sh_tab, sc_tab, wq, keys, tb):
    D = dims.D
    H, _, n_keys, dk = keys.shape
    keys2 = keys.reshape(H * 2, n_keys, dk).astype(BF16)
    tab_spec = pl.BlockSpec((1, 1, D), lambda i: (dims.mod_row(i, tb), 0, 0))
    s_spec = pl.BlockSpec((H, n_keys, tb), lambda i: (0, 0, i))
    s_f32 = jax.ShapeDtypeStruct((H, n_keys, n_rows), jnp.uint32)
    s_bf16 = jax.ShapeDtypeStruct((H, n_keys, n_rows), BF16)
    return pl.pallas_call(
        functools.partial(_peer_pre_kernel, H),
        grid=(n_rows // tb,),
        in_specs=[
            pl.BlockSpec((tb, D), lambda i: (i, 0)),
            pl.BlockSpec((1, D), lambda i: (0, 0)),
            tab_spec, tab_spec,
            pl.BlockSpec(wq.shape, lambda i: (0, 0)),
            pl.BlockSpec(keys2.shape, lambda i: (0, 0, 0)),
        ],
        out_specs=(pl.BlockSpec((tb, D), lambda i: (i, 0)), s_spec, s_spec, s_spec, s_spec),
        out_shape=(jax.ShapeDtypeStruct((n_rows, D), BF16), s_f32, s_f32, s_bf16, s_bf16),
        compiler_params=_cparams("parallel"),
        name="peer_pre",
    )(X, g.reshape(1, D), sh_tab, sc_tab, wq.astype(BF16), keys2)


PEER_SECOND_K = 256


def _peer_gate_tiles(nr_ref, p1_ref, r2_ref, p2_ref, chunk, per_chunk, g_ref, slot):
    n_heads, n_keys, tb = r2_ref.shape
    groups = n_keys // PACKED_ROWS
    lane_chunk = min(tb, 2 * LANES)
    zero = jnp.zeros((PACKED_ROWS, lane_chunk), BF16)
    i1_base = chunk * per_chunk
    tiles = []
    for ii in range(per_chunk):
        for lo in range(0, tb, lane_chunk):
            lanes = slice(lo, lo + lane_chunk)
            g = [None] * groups
            for hh in range(n_heads):
                nrow = _bf16_row_tile(nr_ref, hh, i1_base, ii, lanes)
                prow = _bf16_row_tile(p1_ref, hh, i1_base, ii, lanes)
                for k in range(groups):
                    rows = slice(k * PACKED_ROWS, (k + 1) * PACKED_ROWS)
                    w = jnp.where(r2_ref[hh, rows, lanes] < nrow, p2_ref[hh, rows, lanes], zero) * prow
                    g[k] = w if g[k] is None else g[k] + w
            for k in range(groups):
                rows = slice(ii * n_keys + k * PACKED_ROWS, ii * n_keys + (k + 1) * PACKED_ROWS)
                g_ref[slot, rows, lanes] = g[k]
                tiles.append(g[k])
    return tiles


def _anchored(ref, tiles):
    w = tiles[0].shape[1]
    n_rg, n_kt = ref.shape[0] // PACKED_ROWS, ref.shape[1] // w
    cols = []
    for kt in range(n_kt):
        blocks = []
        for rg in range(n_rg):
            tile = tiles[(kt * n_rg + rg) * len(tiles) // (n_kt * n_rg)]
            blk = ref[rg * PACKED_ROWS:(rg + 1) * PACKED_ROWS, kt * w:(kt + 1) * w]
            blocks.append(blk + jnp.minimum(tile, 0))
        cols.append(jnp.concatenate(blocks, axis=0))
    return jnp.concatenate(cols, axis=1)


def _peer_main_kernel(n_chunks, final, h_ref, nr_ref, p1_ref, r2_ref, p2_ref, u_ref, vt_ref, x_ref,
                      gate_ref, fg_ref, o_ref, g_ref, act_ref, p_ref, acc_ref):
    e = pl.program_id(1)
    n_keys = r2_ref.shape[1]
    ec = u_ref.shape[0]
    per_chunk = ec // n_keys
    gate_args = (nr_ref, p1_ref, r2_ref, p2_ref)

    @pl.when(e == 0)
    def _():
        acc_ref[...] = jnp.zeros_like(acc_ref)
        _peer_gate_tiles(*gate_args, 0, per_chunk, g_ref, 0)

    slot = e % 2
    tiles = _peer_gate_tiles(*gate_args, jnp.minimum(e + 1, n_chunks - 1), per_chunk, g_ref, 1 - slot)
    half = len(tiles) * 11 // 16
    act_ref[...] = _dot_nt(_anchored(u_ref, tiles[:half]), h_ref[...])
    for r in range(0, ec, PACKED_ROWS):
        rows = slice(r, r + PACKED_ROWS)
        p_ref[rows, :] = g_ref[slot, rows, :] * _gelu_exact(act_ref[rows, :]).astype(BF16)
    acc_ref[...] += _dot(_anchored(vt_ref, tiles[half:]), p_ref[...])

    @pl.when(e == n_chunks - 1)
    def _():
        out = x_ref[...] + gate_ref[0] * acc_ref[...].T
        if final:
            out = out * lax.rsqrt(jnp.mean(out * out, axis=-1, keepdims=True) + EPS) * fg_ref[...]
        o_ref[...] = out


def _peer_layer(dims, X, n_rows, g, tabs, wq, keys, u, v, final_g):
    D = dims.D
    H, _, n_keys, _ = keys.shape
    n_exp = u.shape[0]
    h2, nr, p1, r2, p2 = _peer_pre(dims, X, n_rows, g, tabs[3], tabs[4], wq, keys, dims.tile(n_rows, 256))
    tb = dims.tile(n_rows, 512)
    ec = min(1024, n_exp)
    assert ec % PEER_SECOND_K == 0 and PEER_SECOND_K % n_keys == 0 and n_keys % PACKED_ROWS == 0
    assert (ec // n_keys) % SUBLANES == 0
    n_chunks = n_exp // ec
    u_bf = u.astype(BF16)
    vt_bf = v.astype(BF16).T
    final = final_g is not None
    fg = (final_g if final else jnp.ones((D,), F32)).reshape(1, D)
    s_spec = pl.BlockSpec((H, n_keys, tb), lambda i, e: (0, 0, i))
    return pl.pallas_call(
        functools.partial(_peer_main_kernel, n_chunks, final),
        grid=(n_rows // tb, n_chunks),
        in_specs=[
            pl.BlockSpec((tb, D), lambda i, e: (i, 0)),
            s_spec, s_spec, s_spec, s_spec,
            pl.BlockSpec((ec, D), lambda i, e: (e, 0)),
            pl.BlockSpec((D, ec), lambda i, e: (0, e)),
            pl.BlockSpec((tb, D), lambda i, e: (i, 0)),
            pl.BlockSpec((1, 1, D), lambda i, e: (dims.mod_row(i, tb), 0, 0)),
            pl.BlockSpec((1, D), lambda i, e: (0, 0)),
        ],
        out_specs=pl.BlockSpec((tb, D), lambda i, e: (i, 0)),
        out_shape=jax.ShapeDtypeStruct((n_rows, D), F32),
        scratch_shapes=[pltpu.VMEM((2, ec, tb), BF16), pltpu.VMEM((ec, tb), F32), pltpu.VMEM((ec, tb), BF16),
                        pltpu.VMEM((D, tb), F32)],
        compiler_params=_cparams("parallel", "arbitrary"),
        name="peer_main",
    )(h2, nr, p1, r2, p2, u_bf, vt_bf, X, tabs[5], fg)


def kernel(x, c, ctx, c_ctx, w_mod, b_mod, norm_g, four_w_out, diff_w_in, diff_lam, diff_norm_g, diff_w_out,
           ssd_w_in, ssd_conv_w, ssd_conv_b, ssd_a_log, ssd_dt_bias, ssd_d, ssd_norm_g, ssd_w_out, hgrn_w_in,
           hgrn_lb, hgrn_norm_g, hgrn_w_out, peer_wq, peer_keys, peer_u, peer_v, final_g):
    B, L, D = x.shape
    C = ctx.shape[1]
    depth = w_mod.shape[0]
    n_mixers = 4
    dims = Dims(B, L, C, D)
    assert B + 1 <= MOD_ROWS

    X = jnp.concatenate([x.reshape(B * L, D), ctx.reshape(B * C, D)], axis=0)
    cc = jnp.concatenate([c, c_ctx[None, :], jnp.zeros((MOD_ROWS - B - 1, D), c.dtype)], axis=0)
    mods = _mod_tables(cc, w_mod, b_mod)

    for i in range(depth):
        last = i == depth - 1
        kind, j = i % n_mixers, i // n_mixers
        tabs = [mods[i, :, k * D:(k + 1) * D].reshape(MOD_ROWS, 1, D) for k in range(6)]
        g1, g2 = norm_g[i, 0], norm_g[i, 1]
        if kind == 0:
            X = _fourier_layer(dims, X, g1, tabs, four_w_out[j])
        elif kind == 1:
            X = _diff_layer(dims, X, g1, tabs, diff_w_in[j], diff_lam[j], diff_norm_g[j], diff_w_out[j], i)
        elif kind == 2:
            X = _ssd_layer(dims, X, g1, tabs, ssd_w_in[j], ssd_conv_w[j], ssd_conv_b[j], ssd_a_log[j],
                           ssd_dt_bias[j], ssd_d[j], ssd_norm_g[j], ssd_w_out[j])
        else:
            X = _hgrn_layer(dims, X, g1, tabs, hgrn_w_in[j], hgrn_lb, i, hgrn_norm_g[j], hgrn_w_out[j],
                            not last)
        n_rows = dims.TL if last else dims.T
        X = _peer_layer(dims, X, n_rows, g2, tabs, peer_wq[i], peer_keys[i], peer_u[i], peer_v[i],
                        final_g if last else None)
    return X[:B * L].reshape(B, L, D)
```
